```python
import jax, jax.numpy as jnp
from jax import lax
import numpy as np

D_MODEL = 4096
BATCH = 2
SEQ = 8192
DEPTH = 2

D_CONV = D_MODEL // 2
CONV_WIDTH = 3
N_HEADS = 8
DQK = D_MODEL // 16
DV = D_MODEL // 8
D_QK = N_HEADS * DQK
D_V = N_HEADS * DV
CHUNK = 64
GATE_SOFTCAP = 15.0
N_IN = 3 * D_CONV + 2 * D_QK + 2 * D_V + 2 * N_HEADS + 2 * D_MODEL
D_FF = 14336
N_EXPERTS = 8
TOP_K = 2
D_FF_EXPERT = D_MODEL
N_DENSE = (DEPTH + 1) // 2
N_MOE = DEPTH // 2
EPS = 1e-6

kernel_name = "hybrid_conv_mlstm_gated_moe_trunk"


def _rmsnorm(x, g):
    xf = x.astype(jnp.float32)
    y = xf * lax.rsqrt(jnp.mean(xf * xf, axis=-1, keepdims=True) + EPS)
    return (y * g.astype(jnp.float32)).astype(x.dtype)


def _softcap(z):
    return GATE_SOFTCAP * jnp.tanh(z / GATE_SOFTCAP)


def _causal_short_conv(u, w, b):
    y = lax.conv_general_dilated(
        u, w[:, None, :].astype(u.dtype), window_strides=(1,),
        padding=[(CONV_WIDTH - 1, 0)], dimension_numbers=('NWC', 'WIO', 'NWC'),
        feature_group_count=u.shape[-1])
    return y + b.astype(u.dtype)


def _to_chunks(t, nc):
    bsz = t.shape[0]
    t = t.reshape((bsz, nc, CHUNK) + t.shape[2:])
    return jnp.transpose(t, (1, 0, 3, 2) + tuple(range(4, t.ndim)))


def _mlstm_chunkwise(q, k, v, i_pre, logf):
    bsz, seq = q.shape[0], q.shape[1]
    nc = seq // CHUNK
    xs = (_to_chunks(q, nc), _to_chunks(k * (DQK ** -0.5), nc), _to_chunks(v, nc),
          _to_chunks(i_pre, nc), _to_chunks(logf, nc))
    causal = jnp.tril(jnp.ones((CHUNK, CHUNK), dtype=bool))

    def step(carry, inp):
        c_mat, n_vec, m_prev = carry
        qc, kc, vc, ic, fc = inp
        b = jnp.cumsum(fc, axis=-1)
        d_log = jnp.where(causal, b[..., :, None] - b[..., None, :] + ic[..., None, :], -jnp.inf)
        inter = b + m_prev[..., None]
        m_t = jnp.maximum(inter, jnp.max(d_log, axis=-1))
        w_inter = jnp.exp(inter - m_t)
        s_qk = jnp.einsum('bhtd,bhsd->bhts', qc, kc) * jnp.exp(d_log - m_t[..., None])
        num = (w_inter[..., None] * jnp.einsum('bhtd,bhde->bhte', qc, c_mat)
               + jnp.einsum('bhts,bhse->bhte', s_qk, vc))
        den = w_inter * jnp.einsum('bhtd,bhd->bht', qc, n_vec) + jnp.sum(s_qk, axis=-1)
        h = num / jnp.maximum(jnp.abs(den), jnp.exp(-m_t))[..., None]
        m_new = m_t[..., -1]
        b_last = b[..., -1]
        decay = jnp.exp(b_last + m_prev - m_new)
        w_s = jnp.exp(b_last[..., None] - b + ic - m_new[..., None])
        kw = kc * w_s[..., None]
        c_new = decay[..., None, None] * c_mat + jnp.einsum('bhsd,bhse->bhde', kw, vc)
        n_new = decay[..., None] * n_vec + jnp.sum(kw, axis=2)
        return (c_new, n_new, m_new), h

    init = (jnp.zeros((bsz, N_HEADS, DQK, DV), jnp.float32),
            jnp.zeros((bsz, N_HEADS, DQK), jnp.float32),
            jnp.zeros((bsz, N_HEADS), jnp.float32))
    _, hs = lax.scan(step, init, xs)
    return jnp.transpose(hs, (1, 0, 3, 2, 4)).reshape(bsz, seq, N_HEADS, DV)


def _hybrid_mixer(xn, w_in, conv_w, conv_b, i_bias, f_bias, mh_norm, p_conv, p_mlstm, w_out):
    bsz, seq, _ = xn.shape
    sizes = (D_CONV, D_CONV, D_CONV, D_QK, D_QK, D_V, D_V, N_HEADS, N_HEADS, D_MODEL, D_MODEL)
    idx = np.cumsum(sizes)[:-1].tolist()
    (b_g, c_g, v_c, q, k, v, o_pre, i_pre, f_pre, g_a, g_b) = jnp.split(xn @ w_in, idx, axis=-1)
    y_a = b_g * _causal_short_conv(c_g * v_c, conv_w, conv_b)
    f32 = jnp.float32
    i_g = _softcap(i_pre.astype(f32) + i_bias.astype(f32))
    logf = jax.nn.log_sigmoid(_softcap(f_pre.astype(f32) + f_bias.astype(f32)))
    h = _mlstm_chunkwise(q.astype(f32).reshape(bsz, seq, N_HEADS, DQK),
                         k.astype(f32).reshape(bsz, seq, N_HEADS, DQK),
                         v.astype(f32).reshape(bsz, seq, N_HEADS, DV), i_g, logf)
    h = h * lax.rsqrt(jnp.mean(h * h, axis=-1, keepdims=True) + EPS)
    h = h * mh_norm.astype(f32).reshape(N_HEADS, DV)
    y_b = (jax.nn.sigmoid(o_pre.astype(f32)) * h.reshape(bsz, seq, D_V)).astype(xn.dtype)
    merged = jax.nn.sigmoid(g_a) * (y_a @ p_conv) + jax.nn.sigmoid(g_b) * (y_b @ p_mlstm)
    return merged @ w_out


def _swiglu(x, w_gate_up, w_down):
    g, u = jnp.split(x @ w_gate_up, 2, axis=-1)
    return (jax.nn.silu(g) * u) @ w_down


def _moe(x, router_w, router_b, w_gate_up, w_down):
    bsz, seq, d = x.shape
    xt = x.reshape(-1, d)
    logits = xt.astype(jnp.float32) @ router_w.astype(jnp.float32) + router_b.astype(jnp.float32)
    top_logits, top_idx = lax.top_k(logits, TOP_K)
    top_w = jax.nn.softmax(top_logits, axis=-1)
    combine = jnp.einsum('nk,nke->ne', top_w, jax.nn.one_hot(top_idx, N_EXPERTS, dtype=jnp.float32))
    out = jnp.zeros(xt.shape, jnp.float32)
    for e in range(N_EXPERTS):
        out = out + combine[:, e:e + 1] * _swiglu(xt, w_gate_up[e], w_down[e]).astype(jnp.float32)
    return out.astype(x.dtype).reshape(bsz, seq, d)


def setup_inputs(seed: int = 0) -> dict:
    key = jax.random.key(seed)
    ks = jax.random.split(key, 20)
    f32 = jnp.float32

    def nrm(k, shape, fan_in):
        return jax.random.normal(k, shape, f32) * (fan_in ** -0.5)

    def gain(k, shape):
        return 1.0 + 0.02 * jax.random.normal(k, shape, f32)

    return {
        'x': jax.random.normal(ks[0], (BATCH, SEQ, D_MODEL), f32),
        'ln_mix': gain(ks[1], (DEPTH, D_MODEL)),
        'ln_ffn': gain(ks[2], (DEPTH, D_MODEL)),
        'w_in': nrm(ks[3], (DEPTH, D_MODEL, N_IN), D_MODEL),
        'conv_w': nrm(ks[4], (DEPTH, CONV_WIDTH, D_CONV), CONV_WIDTH),
        'conv_b': 0.02 * jax.random.normal(ks[5], (DEPTH, D_CONV), f32),
        'i_bias': 0.1 * jax.random.normal(ks[6], (DEPTH, N_HEADS), f32),
        'f_bias': jnp.linspace(3.0, 6.0, N_HEADS, dtype=f32)[None, :]
                  + 0.1 * jax.random.normal(ks[7], (DEPTH, N_HEADS), f32),
        'mh_norm': gain(ks[8], (DEPTH, D_V)),
        'p_conv': nrm(ks[9], (DEPTH, D_CONV, D_MODEL), D_CONV),
        'p_mlstm': nrm(ks[10], (DEPTH, D_V, D_MODEL), D_V),
        'w_out': nrm(ks[11], (DEPTH, D_MODEL, D_MODEL), D_MODEL),
        'ffn_w_gate_up': nrm(ks[12], (N_DENSE, D_MODEL, 2 * D_FF), D_MODEL),
        'ffn_w_down': nrm(ks[13], (N_DENSE, D_FF, D_MODEL), D_FF),
        'router_w': nrm(ks[14], (N_MOE, D_MODEL, N_EXPERTS), D_MODEL),
        'router_b': 0.01 * jax.random.normal(ks[15], (N_MOE, N_EXPERTS), f32),
        'exp_w_gate_up': nrm(ks[16], (N_MOE, N_EXPERTS, D_MODEL, 2 * D_FF_EXPERT), D_MODEL),
        'exp_w_down': nrm(ks[17], (N_MOE, N_EXPERTS, D_FF_EXPERT, D_MODEL), D_FF_EXPERT),
        'final_norm': gain(ks[18], (D_MODEL,)),
    }


def reference(x, ln_mix, ln_ffn, w_in, conv_w, conv_b, i_bias, f_bias, mh_norm, p_conv, p_mlstm,
              w_out, ffn_w_gate_up, ffn_w_down, router_w, router_b, exp_w_gate_up, exp_w_down,
              final_norm):
    for layer in range(DEPTH):
        x = x + _hybrid_mixer(_rmsnorm(x, ln_mix[layer]), w_in[layer], conv_w[layer], conv_b[layer],
                              i_bias[layer], f_bias[layer], mh_norm[layer], p_conv[layer],
                              p_mlstm[layer], w_out[layer])
        h = _rmsnorm(x, ln_ffn[layer])
        j = layer // 2
        if layer % 2 == 0:
            x = x + _swiglu(h, ffn_w_gate_up[j], ffn_w_down[j])
        else:
            x = x + _moe(h, router_w[j], router_b[j], exp_w_gate_up[j], exp_w_down[j])
    return _rmsnorm(x, final_norm)
```

```python
import functools

import jax
import jax.numpy as jnp
from jax import lax
from jax.experimental import pallas as pl
from jax.experimental.pallas import tpu as pltpu

F32 = jnp.float32
BF16 = jnp.bfloat16

EPS = 1e-6
GATE_SOFTCAP = 15.0
N_HEADS = 8
TOP_K = 2

V7X_VMEM_BYTES = 64 * 2**20
LANES = 128
BF16_SUBLANES = 16
MASKED = -1e30

MLSTM_CHUNK = 256
N_GATE_LANES = 2 * N_HEADS


def _tile(dim, pref):
    t = min(dim, pref)
    while dim % t:
        t //= 2
    return t


def _params(semantics, block_bytes, scratch_bytes=0, temp_bytes=0):
    need = 2 * block_bytes + scratch_bytes + temp_bytes + (4 << 20)
    return pltpu.CompilerParams(
        dimension_semantics=semantics,
        vmem_limit_bytes=int(min(max(need, 32 << 20), V7X_VMEM_BYTES - (6 << 20))))


def _nbytes(shape, dtype):
    n = 1
    for s in shape:
        n *= s
    return n * jnp.dtype(dtype).itemsize


def _rmsnorm_body(x_ref, g_ref, o_ref):
    x = x_ref[...]
    inv = lax.rsqrt(jnp.mean(x * x, axis=-1, keepdims=True) + EPS)
    o_ref[...] = ((x * inv) * g_ref[...]).astype(o_ref.dtype)


def _rmsnorm(x, g, out_dtype):
    n, d = x.shape
    bm = _tile(n, 256)
    return pl.pallas_call(
        _rmsnorm_body,
        out_shape=jax.ShapeDtypeStruct((n, d), out_dtype),
        grid=(n // bm,),
        in_specs=[pl.BlockSpec((bm, d), lambda i: (i, 0)),
                  pl.BlockSpec((1, d), lambda i: (0, 0))],
        out_specs=pl.BlockSpec((bm, d), lambda i: (i, 0)),
        compiler_params=_params(("parallel",), _nbytes((bm, d), F32) * 2),
        name="rmsnorm",
    )(x, g.reshape(1, d))


def _rmsnorm_router_body(x_ref, g_ref, whi_ref, wlo_ref, rb_ref, h_ref, hf_ref, lg_ref):
    x = x_ref[...]
    inv = lax.rsqrt(jnp.mean(x * x, axis=-1, keepdims=True) + EPS)
    h = (x * inv) * g_ref[...]
    h_hi = h.astype(BF16)
    hf_ref[...] = h
    h_ref[...] = h_hi
    h_lo = (h - h_hi.astype(F32)).astype(BF16)
    lg = (jnp.dot(h_hi, whi_ref[...], preferred_element_type=F32)
          + jnp.dot(h_lo, whi_ref[...], preferred_element_type=F32)
          + jnp.dot(h_hi, wlo_ref[...], preferred_element_type=F32))
    lg_ref[...] = lg + rb_ref[...]


def _rmsnorm_router(x, g, router_w, router_b):
    n, d = x.shape
    n_exp = router_w.shape[1]
    bm = _tile(n, 256)
    w = jnp.pad(router_w.astype(F32), ((0, 0), (0, LANES - n_exp)))
    w_hi = w.astype(BF16)
    w_lo = (w - w_hi.astype(F32)).astype(BF16)
    rb = jnp.pad(router_b.astype(F32), (0, LANES - n_exp)).reshape(1, LANES)
    row = lambda i: (i, 0)
    fixed = lambda i: (0, 0)
    return pl.pallas_call(
        _rmsnorm_router_body,
        out_shape=(jax.ShapeDtypeStruct((n, d), BF16), jax.ShapeDtypeStruct((n, d), F32),
                   jax.ShapeDtypeStruct((n, LANES), F32)),
        grid=(n // bm,),
        in_specs=[pl.BlockSpec((bm, d), row), pl.BlockSpec((1, d), fixed),
                  pl.BlockSpec((d, LANES), fixed), pl.BlockSpec((d, LANES), fixed),
                  pl.BlockSpec((1, LANES), fixed)],
        out_specs=(pl.BlockSpec((bm, d), row), pl.BlockSpec((bm, d), row),
                   pl.BlockSpec((bm, LANES), row)),
        compiler_params=_params(("parallel",), _nbytes((bm, d), F32) * 3 + _nbytes((d, LANES), BF16) * 2),
        name="rmsnorm_router",
    )(x, g.reshape(1, d), w_hi, w_lo, rb)


def _matmul_body(x_ref, w_ref, o_ref):
    o_ref[...] = jnp.dot(x_ref[...], w_ref[...], preferred_element_type=F32).astype(o_ref.dtype)


def _matmul(x, w, out_dtype, bm_pref=1024, bn_pref=1024):
    m, k = x.shape
    n = w.shape[1]
    bm, bn = _tile(m, bm_pref), _tile(n, bn_pref)
    blocks = _nbytes((bm, k), x.dtype) + _nbytes((k, bn), w.dtype) + _nbytes((bm, bn), out_dtype)
    return pl.pallas_call(
        _matmul_body,
        out_shape=jax.ShapeDtypeStruct((m, n), out_dtype),
        grid=(m // bm, n // bn),
        in_specs=[pl.BlockSpec((bm, k), lambda i, j: (i, 0)),
                  pl.BlockSpec((k, bn), lambda i, j: (0, j))],
        out_specs=pl.BlockSpec((bm, bn), lambda i, j: (i, j)),
        compiler_params=_params(("parallel", "parallel"), blocks, temp_bytes=_nbytes((bm, bn), F32)),
        name="in_proj",
    )(x, w)


def _matmul_residual_body(a_ref, w_ref, x_ref, o_ref):
    o_ref[...] = x_ref[...] + jnp.dot(a_ref[...], w_ref[...], preferred_element_type=F32)


def _matmul_residual(a, w, x, bm_pref=1024, bn_pref=512):
    m, k = a.shape
    n = w.shape[1]
    bm, bn = _tile(m, bm_pref), _tile(n, bn_pref)
    blocks = _nbytes((bm, k), a.dtype) + _nbytes((k, bn), w.dtype) + 2 * _nbytes((bm, bn), F32)
    return pl.pallas_call(
        _matmul_residual_body,
        out_shape=jax.ShapeDtypeStruct((m, n), F32),
        grid=(m // bm, n // bn),
        in_specs=[pl.BlockSpec((bm, k), lambda i, j: (i, 0)),
                  pl.BlockSpec((k, bn), lambda i, j: (0, j)),
                  pl.BlockSpec((bm, bn), lambda i, j: (i, j))],
        out_specs=pl.BlockSpec((bm, bn), lambda i, j: (i, j)),
        compiler_params=_params(("parallel", "parallel"), blocks, temp_bytes=_nbytes((bm, bn), F32)),
        name="out_proj",
    )(a, w, x)


def _matmul_residual_ktiled_body(a_ref, w_ref, x_ref, o_ref):
    part = jnp.dot(a_ref[...], w_ref[...], preferred_element_type=F32)

    @pl.when(pl.program_id(2) == 0)
    def _():
        o_ref[...] = x_ref[...] + part

    @pl.when(pl.program_id(2) != 0)
    def _():
        o_ref[...] += part


def _matmul_residual_ktiled(a, w, x, bm_pref=1024, bn_pref=1024, bk_pref=2048):
    m, k = a.shape
    n = w.shape[1]
    bm, bn, bk = _tile(m, bm_pref), _tile(n, bn_pref), _tile(k, bk_pref)
    blocks = _nbytes((bm, bk), a.dtype) + _nbytes((bk, bn), w.dtype) + 2 * _nbytes((bm, bn), F32)
    return pl.pallas_call(
        _matmul_residual_ktiled_body,
        out_shape=jax.ShapeDtypeStruct((m, n), F32),
        grid=(m // bm, n // bn, k // bk),
        in_specs=[pl.BlockSpec((bm, bk), lambda i, j, kk: (i, kk)),
                  pl.BlockSpec((bk, bn), lambda i, j, kk: (kk, j)),
                  pl.BlockSpec((bm, bn), lambda i, j, kk: (i, j))],
        out_specs=pl.BlockSpec((bm, bn), lambda i, j, kk: (i, j)),
        compiler_params=_params(("parallel", "parallel", "arbitrary"), blocks,
                                temp_bytes=_nbytes((bm, bn), F32)),
        name="ffn_down",
    )(a, w, x)


def _swiglu_up_body(h_ref, wg_ref, wu_ref, o_ref):
    h = h_ref[...]
    g = jnp.dot(h, wg_ref[...], preferred_element_type=F32)
    u = jnp.dot(h, wu_ref[...], preferred_element_type=F32)
    o_ref[...] = (g * jax.nn.sigmoid(g) * u).astype(o_ref.dtype)


def _swiglu_up(h, w_gate_up, bm_pref=1024, bn_pref=512):
    m, k = h.shape
    f = w_gate_up.shape[1] // 2
    bm, bn = _tile(m, bm_pref), _tile(f, bn_pref)
    up0 = f // bn
    blocks = _nbytes((bm, k), h.dtype) + 2 * _nbytes((k, bn), w_gate_up.dtype) + _nbytes((bm, bn), BF16)
    return pl.pallas_call(
        _swiglu_up_body,
        out_shape=jax.ShapeDtypeStruct((m, f), BF16),
        grid=(m // bm, f // bn),
        in_specs=[pl.BlockSpec((bm, k), lambda i, j: (i, 0)),
                  pl.BlockSpec((k, bn), lambda i, j: (0, j)),
                  pl.BlockSpec((k, bn), lambda i, j: (0, j + up0))],
        out_specs=pl.BlockSpec((bm, bn), lambda i, j: (i, j)),
        compiler_params=_params(("parallel", "parallel"), blocks, temp_bytes=3 * _nbytes((bm, bn), F32)),
        name="ffn_up",
    )(h, w_gate_up, w_gate_up)


_COL_V, _COL_O, _COL_GA, _COL_GB = 0, 1, 2, 3
_COL_Q, _COL_K, _COL_BG, _COL_CG, _COL_VC = 8, 9, 10, 11, 12


def _split_in_proj(w, d):
    dc = d // 2
    sizes = (dc, dc, dc, dc, dc, d, d, N_HEADS, N_HEADS, d, d)
    offs = [0]
    for s in sizes:
        offs.append(offs[-1] + s)
    col = lambda i: w[:, offs[i]:offs[i + 1]]
    b_g, c_g, v_c, q, k, v, o_pre, i_pre, f_pre, g_a, g_b = (col(i) for i in range(11))
    w_main = jnp.concatenate([v, o_pre, g_a, g_b, q, k, b_g, c_g, v_c], axis=1).astype(BF16)
    w_gate = jnp.concatenate([i_pre, f_pre], axis=1)
    w_gate = jnp.pad(w_gate, ((0, 0), (0, LANES - N_GATE_LANES))).astype(BF16)
    return w_main, w_gate


def _gates_body(x_ref, w_ref, b_ref, gc_ref, gr_ref):
    z = jnp.dot(x_ref[...], w_ref[...], preferred_element_type=F32) + b_ref[...]
    sc = GATE_SOFTCAP * jnp.tanh(z / GATE_SOFTCAP)
    logf = jnp.minimum(sc, 0.0) - jnp.log1p(jnp.exp(-jnp.abs(sc)))
    rows = z.shape[0]
    row = lax.broadcasted_iota(jnp.int32, z.shape, 0)
    cum = logf
    shift = 1
    while shift < rows:
        cum = cum + jnp.where(row >= shift, pltpu.roll(cum, shift, 0), 0.0)
        shift *= 2
    lane = lax.broadcasted_iota(jnp.int32, z.shape, 1)
    gc = jnp.where(lane < N_HEADS, sc, cum)
    gc_ref[...] = gc
    gr_ref[0] = gc.T[:N_GATE_LANES, :]


def _gates(xn, w_gate, i_bias, f_bias, chunk):
    n, d = xn.shape
    bias = jnp.pad(jnp.concatenate([i_bias, f_bias]).astype(F32), (0, LANES - N_GATE_LANES))
    return pl.pallas_call(
        _gates_body,
        out_shape=(jax.ShapeDtypeStruct((n, LANES), F32),
                   jax.ShapeDtypeStruct((n // chunk, N_GATE_LANES, chunk), F32)),
        grid=(n // chunk,),
        in_specs=[pl.BlockSpec((chunk, d), lambda i: (i, 0)),
                  pl.BlockSpec((d, LANES), lambda i: (0, 0)),
                  pl.BlockSpec((1, LANES), lambda i: (0, 0))],
        out_specs=(pl.BlockSpec((chunk, LANES), lambda i: (i, 0)),
                   pl.BlockSpec((1, N_GATE_LANES, chunk), lambda i: (i, 0, 0))),
        compiler_params=_params(("parallel",), _nbytes((chunk, d), BF16) + _nbytes((d, LANES), BF16)),
        name="mlstm_gates",
    )(xn, w_gate, bias.reshape(1, LANES))


def _mlstm_body(q_ref, k_ref, v_ref, o_ref, gc_ref, gr_ref, g_ref, y_ref, c_scr, n_scr, m_scr,
                *, dqk, dv):
    @pl.when(pl.program_id(1) == 0)
    def _():
        c_scr[...] = jnp.zeros_like(c_scr)
        n_scr[...] = jnp.zeros_like(n_scr)
        m_scr[...] = jnp.zeros_like(m_scr)

    chunk = q_ref.shape[0]
    t_idx = lax.broadcasted_iota(jnp.int32, (chunk, chunk), 0)
    s_idx = lax.broadcasted_iota(jnp.int32, (chunk, chunk), 1)
    causal = s_idx <= t_idx
    gc = gc_ref[...]
    gr = gr_ref[0]
    scale = dqk ** -0.5
    for h in range(N_HEADS):
        q = q_ref[:, h * dqk:(h + 1) * dqk]
        kf = k_ref[:, h * dqk:(h + 1) * dqk].astype(F32) * scale
        v = v_ref[:, h * dv:(h + 1) * dv]
        i_col, b_col = gc[:, h:h + 1], gc[:, N_HEADS + h:N_HEADS + h + 1]
        i_row, b_row = gr[h:h + 1, :], gr[N_HEADS + h:N_HEADS + h + 1, :]
        m_prev = m_scr[h]
        d_log = jnp.where(causal, b_col - b_row + i_row, MASKED)
        inter = b_col + m_prev
        m_t = jnp.maximum(inter, jnp.max(d_log, axis=-1, keepdims=True))
        w_inter = jnp.exp(inter - m_t)
        s_qk = lax.dot_general(q, kf.astype(BF16), (((1,), (1,)), ((), ())),
                               preferred_element_type=F32) * jnp.exp(d_log - m_t)
        c_mat = c_scr[h]
        n_vec = n_scr[h]
        num = (w_inter * jnp.dot(q, c_mat.astype(BF16), preferred_element_type=F32)
               + jnp.dot(s_qk.astype(BF16), v, preferred_element_type=F32))
        den = (w_inter * jnp.sum(q.astype(F32) * n_vec, axis=-1, keepdims=True)
               + jnp.sum(s_qk, axis=-1, keepdims=True))
        hid = num / jnp.maximum(jnp.abs(den), jnp.exp(-m_t))
        m_new = m_t[chunk - 1:chunk, :]
        b_last = b_col[chunk - 1:chunk, :]
        decay = jnp.exp(b_last + m_prev - m_new)
        kw = kf * jnp.exp(b_last - b_col + i_col - m_new)
        c_scr[h] = decay * c_mat + lax.dot_general(kw.astype(BF16), v, (((0,), (0,)), ((), ())),
                                                   preferred_element_type=F32)
        n_scr[h] = decay * n_vec + jnp.sum(kw, axis=0, keepdims=True)
        m_scr[h] = m_new
        hid = hid * lax.rsqrt(jnp.mean(hid * hid, axis=-1, keepdims=True) + EPS)
        hid = hid * g_ref[:, h * dv:(h + 1) * dv]
        gate = jax.nn.sigmoid(o_ref[:, h * dv:(h + 1) * dv].astype(F32))
        y_ref[:, h * dv:(h + 1) * dv] = (gate * hid).astype(y_ref.dtype)


def _mlstm(y, gc, gr, mh_norm, batch, seq, d, chunk):
    n = batch * seq
    nc = seq // chunk
    dqk, dv = d // 2 // N_HEADS, d // N_HEADS
    rows = lambda col: (lambda b, c: (b * nc + c, col))
    blocks = (2 * _nbytes((chunk, d // 2), BF16) + 3 * _nbytes((chunk, d), BF16)
              + _nbytes((chunk, LANES), F32))
    state = _nbytes((N_HEADS, dqk, dv), F32)
    return pl.pallas_call(
        functools.partial(_mlstm_body, dqk=dqk, dv=dv),
        out_shape=jax.ShapeDtypeStruct((n, d), BF16),
        grid=(batch, nc),
        in_specs=[pl.BlockSpec((chunk, d // 2), rows(_COL_Q)),
                  pl.BlockSpec((chunk, d // 2), rows(_COL_K)),
                  pl.BlockSpec((chunk, d), rows(_COL_V)),
                  pl.BlockSpec((chunk, d), rows(_COL_O)),
                  pl.BlockSpec((chunk, LANES), rows(0)),
                  pl.BlockSpec((1, N_GATE_LANES, chunk), lambda b, c: (b * nc + c, 0, 0)),
                  pl.BlockSpec((1, d), lambda b, c: (0, 0))],
        out_specs=pl.BlockSpec((chunk, d), rows(0)),
        scratch_shapes=[pltpu.VMEM((N_HEADS, dqk, dv), F32),
                        pltpu.VMEM((N_HEADS, 1, dqk), F32),
                        pltpu.VMEM((N_HEADS, 1, 1), F32)],
        compiler_params=_params(("parallel", "arbitrary"), blocks, scratch_bytes=state,
                                temp_bytes=8 << 20),
        name="mlstm",
    )(y, y, y, y, gc, gr, mh_norm.astype(F32).reshape(1, d))


def _conv_body(bg_ref, cg_ref, vc_ref, cgp_ref, vcp_ref, w_ref, b_ref, o_ref, *, tiles_per_seq):
    u = cg_ref[...].astype(F32) * vc_ref[...].astype(F32)
    keep = (pl.program_id(0) % tiles_per_seq != 0).astype(F32)
    halo = cgp_ref[...].astype(F32) * vcp_ref[...].astype(F32) * keep
    prev1 = halo[BF16_SUBLANES - 1:BF16_SUBLANES, :]
    prev2 = halo[BF16_SUBLANES - 2:BF16_SUBLANES - 1, :]
    row = lax.broadcasted_iota(jnp.int32, u.shape, 0)
    u1 = jnp.where(row == 0, prev1, pltpu.roll(u, 1, 0))
    u2 = jnp.where(row == 0, prev2, jnp.where(row == 1, prev1, pltpu.roll(u, 2, 0)))
    w = w_ref[...]
    conv = w[0:1, :] * u2 + w[1:2, :] * u1 + w[2:3, :] * u + b_ref[...]
    o_ref[...] = (bg_ref[...].astype(F32) * conv).astype(o_ref.dtype)


def _gated_conv(y, conv_w, conv_b, seq, d):
    n = y.shape[0]
    dc = d // 2
    bm, bc = _tile(seq, 512), _tile(dc, 512)
    ncol = dc // bc
    halo_rows = bm // BF16_SUBLANES
    cur = lambda col: (lambda i, j: (i, col * ncol + j))
    prev = lambda col: (lambda i, j: (jnp.maximum(i * halo_rows - 1, 0), col * ncol + j))
    return pl.pallas_call(
        functools.partial(_conv_body, tiles_per_seq=seq // bm),
        out_shape=jax.ShapeDtypeStruct((n, dc), BF16),
        grid=(n // bm, ncol),
        in_specs=[pl.BlockSpec((bm, bc), cur(_COL_BG)),
                  pl.BlockSpec((bm, bc), cur(_COL_CG)),
                  pl.BlockSpec((bm, bc), cur(_COL_VC)),
                  pl.BlockSpec((BF16_SUBLANES, bc), prev(_COL_CG)),
                  pl.BlockSpec((BF16_SUBLANES, bc), prev(_COL_VC)),
                  pl.BlockSpec((conv_w.shape[0], bc), lambda i, j: (0, j)),
                  pl.BlockSpec((1, bc), lambda i, j: (0, j))],
        out_specs=pl.BlockSpec((bm, bc), lambda i, j: (i, j)),
        compiler_params=_params(("parallel", "parallel"), 4 * _nbytes((bm, bc), BF16),
                                temp_bytes=8 * _nbytes((bm, bc), F32)),
        name="gated_conv",
    )(y, y, y, y, y, conv_w.astype(F32), conv_b.astype(F32).reshape(1, dc))


def _merge_body(ya_ref, yb_ref, pc_ref, pm_ref, ga_ref, gb_ref, o_ref):
    a = jnp.dot(ya_ref[...], pc_ref[...], preferred_element_type=F32)
    b = jnp.dot(yb_ref[...], pm_ref[...], preferred_element_type=F32)
    merged = (jax.nn.sigmoid(ga_ref[...].astype(F32)) * a
              + jax.nn.sigmoid(gb_ref[...].astype(F32)) * b)
    o_ref[...] = merged.astype(o_ref.dtype)


def _merge(y_a, y_b, p_conv, p_mlstm, y, d, bm_pref=1024, bn_pref=512):
    n = y_a.shape[0]
    bm, bn = _tile(n, bm_pref), _tile(d, bn_pref)
    ncol = d // bn
    blocks = (_nbytes((bm, d // 2), BF16) + _nbytes((bm, d), BF16) + _nbytes((d // 2, bn), BF16)
              + _nbytes((d, bn), BF16) + 3 * _nbytes((bm, bn), BF16))
    return pl.pallas_call(
        _merge_body,
        out_shape=jax.ShapeDtypeStruct((n, d), BF16),
        grid=(n // bm, ncol),
        in_specs=[pl.BlockSpec((bm, d // 2), lambda i, j: (i, 0)),
                  pl.BlockSpec((bm, d), lambda i, j: (i, 0)),
                  pl.BlockSpec((d // 2, bn), lambda i, j: (0, j)),
                  pl.BlockSpec((d, bn), lambda i, j: (0, j)),
                  pl.BlockSpec((bm, bn), lambda i, j: (i, _COL_GA * ncol + j)),
                  pl.BlockSpec((bm, bn), lambda i, j: (i, _COL_GB * ncol + j))],
        out_specs=pl.BlockSpec((bm, bn), lambda i, j: (i, j)),
        compiler_params=_params(("parallel", "parallel"), blocks, temp_bytes=3 * _nbytes((bm, bn), F32)),
        name="merge",
    )(y_a, y_b, p_conv, p_mlstm, y, y)


def _hybrid_mixer(x, ln, w_in, conv_w, conv_b, i_bias, f_bias, mh_norm, p_conv, p_mlstm, w_out,
                  batch, seq):
    n, d = x.shape
    chunk = _tile(seq, MLSTM_CHUNK)
    w_main, w_gate = _split_in_proj(w_in, d)
    xn = _rmsnorm(x, ln, BF16)
    y = _matmul(xn, w_main, BF16)
    gc, gr = _gates(xn, w_gate, i_bias, f_bias, chunk)
    y_b = _mlstm(y, gc, gr, mh_norm, batch, seq, d, chunk)
    y_a = _gated_conv(y, conv_w, conv_b, seq, d)
    merged = _merge(y_a, y_b, p_conv.astype(BF16), p_mlstm.astype(BF16), y, d)
    return _matmul_residual(merged, w_out.astype(BF16), x)


_R_E1, _R_E2, _R_RANK1, _R_RANK2, _R_W1, _R_W2 = 0, 1, 2, 3, 4, 5


def _route_body(lg_ref, r_ref, cnt_ref, carry, *, n_exp):
    @pl.when(pl.program_id(0) == 0)
    def _():
        carry[...] = jnp.zeros_like(carry)

    lg = lg_ref[...]
    bt = lg.shape[0]
    lane_i = lax.broadcasted_iota(jnp.int32, lg.shape, 1)
    lane = lane_i.astype(F32)
    x1 = jnp.where(lane_i < n_exp, lg, MASKED)
    m1 = jnp.max(x1, axis=-1, keepdims=True)
    e1 = jnp.min(jnp.where(x1 == m1, lane, float(LANES)), axis=-1, keepdims=True)
    x2 = jnp.where(lane == e1, MASKED, x1)
    m2 = jnp.max(x2, axis=-1, keepdims=True)
    e2 = jnp.min(jnp.where(x2 == m2, lane, float(LANES)), axis=-1, keepdims=True)
    ex = jnp.exp(m2 - m1)
    w1 = 1.0 / (1.0 + ex)
    w2 = ex / (1.0 + ex)
    hit1, hit2 = lane == e1, lane == e2
    onehot = (hit1 | hit2).astype(F32)
    t_idx = lax.broadcasted_iota(jnp.int32, (bt, bt), 0)
    s_idx = lax.broadcasted_iota(jnp.int32, (bt, bt), 1)
    earlier = (s_idx < t_idx).astype(BF16)
    rank = jnp.dot(earlier, onehot.astype(BF16), preferred_element_type=F32) + carry[...]
    rank1 = jnp.sum(jnp.where(hit1, rank, 0.0), axis=-1, keepdims=True)
    rank2 = jnp.sum(jnp.where(hit2, rank, 0.0), axis=-1, keepdims=True)
    carry[...] += jnp.sum(onehot, axis=0, keepdims=True)
    rec = jnp.zeros_like(lg)
    for idx, val in ((_R_E1, e1), (_R_E2, e2), (_R_RANK1, rank1), (_R_RANK2, rank2),
                     (_R_W1, w1), (_R_W2, w2)):
        rec = jnp.where(lane_i == idx, val, rec)
    r_ref[...] = rec
    cnt_ref[...] = jnp.broadcast_to(carry[...], cnt_ref.shape)


def _route(logits, n_exp):
    n = logits.shape[0]
    bt = _tile(n, 512)
    return pl.pallas_call(
        functools.partial(_route_body, n_exp=n_exp),
        out_shape=(jax.ShapeDtypeStruct((n, LANES), F32), jax.ShapeDtypeStruct((8, LANES), F32)),
        grid=(n // bt,),
        in_specs=[pl.BlockSpec((bt, LANES), lambda i: (i, 0))],
        out_specs=(pl.BlockSpec((bt, LANES), lambda i: (i, 0)),
                   pl.BlockSpec((8, LANES), lambda i: (0, 0))),
        scratch_shapes=[pltpu.VMEM((1, LANES), F32)],
        compiler_params=_params(("arbitrary",), 2 * _nbytes((bt, LANES), F32),
                                temp_bytes=_nbytes((bt, bt), F32)),
        name="moe_route",
    )(logits)


def _row_copy(src_hbm, src_row, dst_ref, dst_row, sem):
    return pltpu.make_async_copy(src_hbm.at[pl.ds(src_row, 1)], dst_ref.at[pl.ds(dst_row, 1)], sem)


def _scatter_rows_body(pos1_ref, pos2_ref, h_hbm, xs_in_hbm, xs_hbm, sem, *, bt):
    del xs_in_hbm
    base = pl.program_id(0) * bt

    def issue(t, carry):
        _row_copy(h_hbm, base + t, xs_hbm, pos1_ref[base + t], sem).start()
        _row_copy(h_hbm, base + t, xs_hbm, pos2_ref[base + t], sem).start()
        return carry

    def drain(t, carry):
        _row_copy(h_hbm, 0, xs_hbm, 0, sem).wait()
        _row_copy(h_hbm, 0, xs_hbm, 0, sem).wait()
        return carry

    lax.fori_loop(0, bt, issue, 0)
    lax.fori_loop(0, bt, drain, 0)


def _scatter_rows(h, pos1, pos2, n_slots):
    n, d = h.shape
    bt = _tile(n, 512)
    xs0 = jnp.zeros((n_slots, d), h.dtype)
    return pl.pallas_call(
        functools.partial(_scatter_rows_body, bt=bt),
        out_shape=jax.ShapeDtypeStruct((n_slots, d), h.dtype),
        grid_spec=pltpu.PrefetchScalarGridSpec(
            num_scalar_prefetch=2,
            grid=(n // bt,),
            in_specs=[pl.BlockSpec(memory_space=pl.ANY), pl.BlockSpec(memory_space=pl.ANY)],
            out_specs=pl.BlockSpec(memory_space=pl.ANY),
            scratch_shapes=[pltpu.SemaphoreType.DMA(())]),
        input_output_aliases={3: 0},
        compiler_params=pltpu.CompilerParams(dimension_semantics=("arbitrary",)),
        name="moe_scatter",
    )(pos1, pos2, h, xs0)


def _moe_up_body(te_ref, nt_ref, x_ref, wg_ref, wu_ref, o_ref, xb_scr):
    p, j = pl.program_id(0), pl.program_id(1)
    del te_ref

    @pl.when(p < nt_ref[0])
    def _():
        @pl.when(j == 0)
        def _():
            xb_scr[...] = x_ref[...].astype(BF16)

        x = xb_scr[...]
        g = jnp.dot(x, wg_ref[0], preferred_element_type=F32)
        u = jnp.dot(x, wu_ref[0], preferred_element_type=F32)
        o_ref[...] = (g * jax.nn.sigmoid(g) * u).astype(o_ref.dtype)

    @pl.when(p >= nt_ref[0])
    def _():
        o_ref[...] = jnp.zeros_like(o_ref)


def _moe_up(xs, w_gate_up, tile_expert, n_tiles, bm, bn_pref=512):
    n_slots, d = xs.shape
    f = w_gate_up.shape[2] // 2
    bn = _tile(f, bn_pref)
    nj = f // bn
    live_j = lambda p, j, nt: jnp.where(p < nt[0], j, nj - 1)
    blocks = _nbytes((bm, d), xs.dtype) + 2 * _nbytes((d, bn), BF16) + _nbytes((bm, bn), BF16)
    return pl.pallas_call(
        _moe_up_body,
        out_shape=jax.ShapeDtypeStruct((n_slots, f), BF16),
        grid_spec=pltpu.PrefetchScalarGridSpec(
            num_scalar_prefetch=2,
            grid=(n_slots // bm, nj),
            in_specs=[pl.BlockSpec((bm, d), lambda p, j, te, nt: (jnp.minimum(p, nt[0] - 1), 0)),
                      pl.BlockSpec((1, d, bn), lambda p, j, te, nt: (te[p], 0, live_j(p, j, nt))),
                      pl.BlockSpec((1, d, bn), lambda p, j, te, nt: (te[p], 0, nj + live_j(p, j, nt)))],
            out_specs=pl.BlockSpec((bm, bn), lambda p, j, te, nt: (p, j)),
            scratch_shapes=[pltpu.VMEM((bm, d), BF16)]),
        compiler_params=_params(("arbitrary", "arbitrary"), blocks, scratch_bytes=_nbytes((bm, d), BF16),
                                temp_bytes=3 * _nbytes((bm, bn), F32)),
        name="moe_up",
    )(tile_expert, n_tiles, xs, w_gate_up, w_gate_up)


def _moe_down_body(te_ref, nt_ref, a_ref, w_ref, o_ref):
    del te_ref

    @pl.when(pl.program_id(0) < nt_ref[0])
    def _():
        o_ref[...] = jnp.dot(a_ref[...], w_ref[0], preferred_element_type=F32)

    @pl.when(pl.program_id(0) >= nt_ref[0])
    def _():
        o_ref[...] = jnp.zeros_like(o_ref)


def _moe_down(act, w_down, tile_expert, n_tiles, bm, bn_pref=512):
    n_slots, f = act.shape
    d = w_down.shape[2]
    bn = _tile(d, bn_pref)
    nj = d // bn
    live_j = lambda p, j, nt: jnp.where(p < nt[0], j, nj - 1)
    blocks = _nbytes((bm, f), BF16) + _nbytes((f, bn), BF16) + _nbytes((bm, bn), F32)
    return pl.pallas_call(
        _moe_down_body,
        out_shape=jax.ShapeDtypeStruct((n_slots, d), F32),
        grid_spec=pltpu.PrefetchScalarGridSpec(
            num_scalar_prefetch=2,
            grid=(n_slots // bm, nj),
            in_specs=[pl.BlockSpec((bm, f), lambda p, j, te, nt: (jnp.minimum(p, nt[0] - 1), 0)),
                      pl.BlockSpec((1, f, bn), lambda p, j, te, nt: (te[p], 0, live_j(p, j, nt)))],
            out_specs=pl.BlockSpec((bm, bn), lambda p, j, te, nt: (p, j))),
        compiler_params=_params(("arbitrary", "arbitrary"), blocks, temp_bytes=_nbytes((bm, bn), F32)),
        name="moe_down",
    )(tile_expert, n_tiles, act, w_down)


def _combine_body(pos1_ref, pos2_ref, r_ref, x_ref, ys_hbm, o_ref, buf, sem, *, bt):
    base = pl.program_id(0) * bt

    def issue(t, carry):
        _row_copy(ys_hbm, pos1_ref[base + t], buf.at[0], t, sem).start()
        _row_copy(ys_hbm, pos2_ref[base + t], buf.at[1], t, sem).start()
        return carry

    def drain(t, carry):
        _row_copy(ys_hbm, 0, buf.at[0], 0, sem).wait()
        _row_copy(ys_hbm, 0, buf.at[1], 0, sem).wait()
        return carry

    lax.fori_loop(0, bt, issue, 0)
    lax.fori_loop(0, bt, drain, 0)
    w1 = r_ref[:, _R_W1:_R_W1 + 1]
    w2 = r_ref[:, _R_W2:_R_W2 + 1]
    o_ref[...] = x_ref[...] + (w1 * buf[0] + w2 * buf[1])


def _combine(x, ys, route, pos1, pos2):
    n, d = x.shape
    bt = _tile(n, 256)
    return pl.pallas_call(
        functools.partial(_combine_body, bt=bt),
        out_shape=jax.ShapeDtypeStruct((n, d), F32),
        grid_spec=pltpu.PrefetchScalarGridSpec(
            num_scalar_prefetch=2,
            grid=(n // bt,),
            in_specs=[pl.BlockSpec((bt, LANES), lambda i, p1, p2: (i, 0)),
                      pl.BlockSpec((bt, d), lambda i, p1, p2: (i, 0)),
                      pl.BlockSpec(memory_space=pl.ANY)],
            out_specs=pl.BlockSpec((bt, d), lambda i, p1, p2: (i, 0)),
            scratch_shapes=[pltpu.VMEM((TOP_K, bt, d), F32), pltpu.SemaphoreType.DMA(())]),
        compiler_params=_params(("arbitrary",), 2 * _nbytes((bt, d), F32),
                                scratch_bytes=_nbytes((TOP_K, bt, d), F32)),
        name="moe_combine",
    )(pos1, pos2, route, x, ys)


def _moe(x, ln, router_w, router_b, w_gate_up, w_down, bm_pref=512):
    n, d = x.shape
    n_exp = router_w.shape[1]
    bm = _tile(n * TOP_K, bm_pref)
    n_tiles_max = (n * TOP_K) // bm + n_exp
    h, h_f32, logits = _rmsnorm_router(x, ln, router_w, router_b)
    del h
    route, counts = _route(logits, n_exp)
    counts = counts[0, :n_exp].astype(jnp.int32)
    sizes = (counts + bm - 1) // bm * bm
    ends = jnp.cumsum(sizes)
    starts = ends - sizes
    e1, e2 = route[:, _R_E1].astype(jnp.int32), route[:, _R_E2].astype(jnp.int32)
    pos1 = starts[e1] + route[:, _R_RANK1].astype(jnp.int32)
    pos2 = starts[e2] + route[:, _R_RANK2].astype(jnp.int32)
    n_tiles = (ends[-1] // bm).astype(jnp.int32).reshape(1)
    tile_start = jnp.arange(n_tiles_max, dtype=jnp.int32) * bm
    tile_expert = jnp.sum(tile_start[:, None] >= ends[None, :], axis=1).astype(jnp.int32)
    tile_expert = jnp.minimum(tile_expert, tile_expert[n_tiles[0] - 1])

    xs = _scatter_rows(h_f32, pos1, pos2, n_tiles_max * bm)
    act = _moe_up(xs, w_gate_up.astype(BF16), tile_expert, n_tiles, bm)
    ys = _moe_down(act, w_down.astype(BF16), tile_expert, n_tiles, bm)
    return _combine(x, ys, route, pos1, pos2)


def kernel(x, ln_mix, ln_ffn, w_in, conv_w, conv_b, i_bias, f_bias, mh_norm, p_conv, p_mlstm, w_out,
           ffn_w_gate_up, ffn_w_down, router_w, router_b, exp_w_gate_up, exp_w_down, final_norm):
    batch, seq, d = x.shape
    depth = ln_mix.shape[0]
    xf = x.reshape(batch * seq, d)
    for layer in range(depth):
        xf = _hybrid_mixer(xf, ln_mix[layer], w_in[layer], conv_w[layer], conv_b[layer],
                           i_bias[layer], f_bias[layer], mh_norm[layer], p_conv[layer],
                           p_mlstm[layer], w_out[layer], batch, seq)
        j = layer // 2
        if layer % 2 == 0:
            h = _rmsnorm(xf, ln_ffn[layer], BF16)
            act = _swiglu_up(h, ffn_w_gate_up[j].astype(BF16))
            xf = _matmul_residual_ktiled(act, ffn_w_down[j].astype(BF16), xf)
        else:
            xf = _moe(xf, ln_ffn[layer], router_w[j], router_b[j], exp_w_gate_up[j], exp_w_down[j])
    return _rmsnorm(xf, final_norm, x.dtype).reshape(batch, seq, d)
```

```python
import functools

import jax
import jax.numpy as jnp
from jax import lax
from jax.experimental import pallas as pl
from jax.experimental.pallas import tpu as pltpu

F32 = jnp.float32
BF16 = jnp.bfloat16

EPS = 1e-6
GATE_SOFTCAP = 15.0
N_HEADS = 8
TOP_K = 2

V7X_VMEM_BYTES = 64 * 2**20
LANES = 128
BF16_SUBLANES = 16
MASKED = -1e30

MLSTM_CHUNK = 256
N_GATE_LANES = 2 * N_HEADS


def _tile(dim, pref):
    t = min(dim, pref)
    while dim % t:
        t //= 2
    return t


def _params(semantics, block_bytes, scratch_bytes=0, temp_bytes=0):
    need = 2 * block_bytes + scratch_bytes + temp_bytes + (4 << 20)
    return pltpu.CompilerParams(
        dimension_semantics=semantics,
        vmem_limit_bytes=int(min(max(need, 32 << 20), V7X_VMEM_BYTES - (6 << 20))))


def _nbytes(shape, dtype):
    n = 1
    for s in shape:
        n *= s
    return n * jnp.dtype(dtype).itemsize


def _rmsnorm_body(x_ref, g_ref, o_ref):
    x = x_ref[...]
    inv = lax.rsqrt(jnp.mean(x * x, axis=-1, keepdims=True) + EPS)
    o_ref[...] = ((x * inv) * g_ref[...]).astype(o_ref.dtype)


def _rmsnorm(x, g, out_dtype):
    n, d = x.shape
    bm = _tile(n, 256)
    return pl.pallas_call(
        _rmsnorm_body,
        out_shape=jax.ShapeDtypeStruct((n, d), out_dtype),
        grid=(n // bm,),
        in_specs=[pl.BlockSpec((bm, d), lambda i: (i, 0)),
                  pl.BlockSpec((1, d), lambda i: (0, 0))],
        out_specs=pl.BlockSpec((bm, d), lambda i: (i, 0)),
        compiler_params=_params(("parallel",), _nbytes((bm, d), F32) * 2),
        name="rmsnorm",
    )(x, g.reshape(1, d))


def _rmsnorm_router_body(x_ref, g_ref, whi_ref, wlo_ref, rb_ref, hf_ref, lg_ref):
    x = x_ref[...]
    inv = lax.rsqrt(jnp.mean(x * x, axis=-1, keepdims=True) + EPS)
    h = (x * inv) * g_ref[...]
    h_hi = h.astype(BF16)
    hf_ref[...] = h
    h_lo = (h - h_hi.astype(F32)).astype(BF16)
    lg = (jnp.dot(h_hi, whi_ref[...], preferred_element_type=F32)
          + jnp.dot(h_lo, whi_ref[...], preferred_element_type=F32)
          + jnp.dot(h_hi, wlo_ref[...], preferred_element_type=F32))
    lg_ref[...] = lg + rb_ref[...]


def _rmsnorm_router(x, g, router_w, router_b):
    n, d = x.shape
    n_exp = router_w.shape[1]
    bm = _tile(n, 256)
    w = jnp.pad(router_w.astype(F32), ((0, 0), (0, LANES - n_exp)))
    w_hi = w.astype(BF16)
    w_lo = (w - w_hi.astype(F32)).astype(BF16)
    rb = jnp.pad(router_b.astype(F32), (0, LANES - n_exp)).reshape(1, LANES)
    row = lambda i: (i, 0)
    fixed = lambda i: (0, 0)
    return pl.pallas_call(
        _rmsnorm_router_body,
        out_shape=(jax.ShapeDtypeStruct((n, d), F32), jax.ShapeDtypeStruct((n, LANES), F32)),
        grid=(n // bm,),
        in_specs=[pl.BlockSpec((bm, d), row), pl.BlockSpec((1, d), fixed),
                  pl.BlockSpec((d, LANES), fixed), pl.BlockSpec((d, LANES), fixed),
                  pl.BlockSpec((1, LANES), fixed)],
        out_specs=(pl.BlockSpec((bm, d), row), pl.BlockSpec((bm, LANES), row)),
        compiler_params=_params(("parallel",), _nbytes((bm, d), F32) * 2 + _nbytes((d, LANES), BF16) * 2),
        name="rmsnorm_router",
    )(x, g.reshape(1, d), w_hi, w_lo, rb)


def _matmul_body(x_ref, w_ref, o_ref, wb_scr):
    @pl.when(pl.program_id(1) == 0)
    def _():
        wb_scr[...] = w_ref[...].astype(BF16)

    o_ref[...] = jnp.dot(x_ref[...], wb_scr[...], preferred_element_type=F32).astype(o_ref.dtype)


def _matmul(x, w3, layer, n_cols, out_dtype, out_block=lambda j: j, bm_pref=1024, bn_pref=512):
    m, k = x.shape
    bm, bn = _tile(m, bm_pref), _tile(n_cols, bn_pref)
    blocks = _nbytes((bm, k), x.dtype) + _nbytes((k, bn), w3.dtype) + _nbytes((bm, bn), out_dtype)
    return pl.pallas_call(
        _matmul_body,
        out_shape=jax.ShapeDtypeStruct((m, n_cols), out_dtype),
        grid=(n_cols // bn, m // bm),
        in_specs=[pl.BlockSpec((bm, k), lambda j, i: (i, 0)),
                  pl.BlockSpec((None, k, bn), lambda j, i: (layer, 0, j))],
        out_specs=pl.BlockSpec((bm, bn), lambda j, i: (i, out_block(j))),
        scratch_shapes=[pltpu.VMEM((k, bn), BF16)],
        compiler_params=_params(("arbitrary", "arbitrary"), blocks, scratch_bytes=_nbytes((k, bn), BF16),
                                temp_bytes=_nbytes((bm, bn), F32)),
        name="in_proj",
    )(x, w3)


def _matmul_residual_body(a_ref, w_ref, x_ref, o_ref):
    o_ref[...] = x_ref[...] + jnp.dot(a_ref[...], w_ref[...], preferred_element_type=F32)


def _matmul_residual(a, w, x, bm_pref=1024, bn_pref=512):
    m, k = a.shape
    n = w.shape[1]
    bm, bn = _tile(m, bm_pref), _tile(n, bn_pref)
    blocks = _nbytes((bm, k), a.dtype) + _nbytes((k, bn), w.dtype) + 2 * _nbytes((bm, bn), F32)
    return pl.pallas_call(
        _matmul_residual_body,
        out_shape=jax.ShapeDtypeStruct((m, n), F32),
        grid=(m // bm, n // bn),
        in_specs=[pl.BlockSpec((bm, k), lambda i, j: (i, 0)),
                  pl.BlockSpec((k, bn), lambda i, j: (0, j)),
                  pl.BlockSpec((bm, bn), lambda i, j: (i, j))],
        out_specs=pl.BlockSpec((bm, bn), lambda i, j: (i, j)),
        compiler_params=_params(("parallel", "parallel"), blocks, temp_bytes=_nbytes((bm, bn), F32)),
        name="out_proj",
    )(a, w, x)


def _matmul_residual_ktiled_body(a_ref, w_ref, x_ref, o_ref):
    part = jnp.dot(a_ref[...], w_ref[...], preferred_element_type=F32)

    @pl.when(pl.program_id(2) == 0)
    def _():
        o_ref[...] = x_ref[...] + part

    @pl.when(pl.program_id(2) != 0)
    def _():
        o_ref[...] += part


def _matmul_residual_ktiled(a, w, x, bm_pref=1024, bn_pref=1024, bk_pref=2048):
    m, k = a.shape
    n = w.shape[1]
    bm, bn, bk = _tile(m, bm_pref), _tile(n, bn_pref), _tile(k, bk_pref)
    blocks = _nbytes((bm, bk), a.dtype) + _nbytes((bk, bn), w.dtype) + 2 * _nbytes((bm, bn), F32)
    return pl.pallas_call(
        _matmul_residual_ktiled_body,
        out_shape=jax.ShapeDtypeStruct((m, n), F32),
        grid=(m // bm, n // bn, k // bk),
        in_specs=[pl.BlockSpec((bm, bk), lambda i, j, kk: (i, kk)),
                  pl.BlockSpec((bk, bn), lambda i, j, kk: (kk, j)),
                  pl.BlockSpec((bm, bn), lambda i, j, kk: (i, j))],
        out_specs=pl.BlockSpec((bm, bn), lambda i, j, kk: (i, j)),
        compiler_params=_params(("parallel", "parallel", "arbitrary"), blocks,
                                temp_bytes=_nbytes((bm, bn), F32)),
        name="ffn_down",
    )(a, w, x)


def _swiglu(h, wg, wu):
    g = jnp.dot(h, wg, preferred_element_type=F32)
    u = jnp.dot(h, wu, preferred_element_type=F32)
    return g * jax.nn.sigmoid(g) * u


def _swiglu_up_body(h_ref, wg_ref, wu_ref, o_ref, wg_scr, wu_scr):
    @pl.when(pl.program_id(1) == 0)
    def _():
        wg_scr[...] = wg_ref[...].astype(BF16)
        wu_scr[...] = wu_ref[...].astype(BF16)

    o_ref[...] = _swiglu(h_ref[...], wg_scr[...], wu_scr[...]).astype(o_ref.dtype)


def _swiglu_up(h, w3, layer, bm_pref=512, bn_pref=512):
    m, k = h.shape
    f = w3.shape[2] // 2
    bm, bn = _tile(m, bm_pref), _tile(f, bn_pref)
    nj = f // bn
    blocks = _nbytes((bm, k), h.dtype) + 2 * _nbytes((k, bn), w3.dtype) + _nbytes((bm, bn), BF16)
    return pl.pallas_call(
        _swiglu_up_body,
        out_shape=jax.ShapeDtypeStruct((m, f), BF16),
        grid=(nj, m // bm),
        in_specs=[pl.BlockSpec((bm, k), lambda j, i: (i, 0)),
                  pl.BlockSpec((None, k, bn), lambda j, i: (layer, 0, j)),
                  pl.BlockSpec((None, k, bn), lambda j, i: (layer, 0, nj + j))],
        out_specs=pl.BlockSpec((bm, bn), lambda j, i: (i, j)),
        scratch_shapes=[pltpu.VMEM((k, bn), BF16), pltpu.VMEM((k, bn), BF16)],
        compiler_params=_params(("arbitrary", "arbitrary"), blocks, scratch_bytes=2 * _nbytes((k, bn), BF16),
                                temp_bytes=3 * _nbytes((bm, bn), F32)),
        name="ffn_up",
    )(h, w3, w3)


_COL_V, _COL_O = 0, 1
_COL_Q, _COL_K, _COL_BG, _COL_CG, _COL_VC = 4, 5, 6, 7, 8
_COL_GA, _COL_GB = 0, 1
_N_MAIN_UNITS = 9


def _main_out_block(nb):
    return lambda j: jnp.where(j < 3 * nb, j + 6 * nb, jnp.where(j < 5 * nb, j + nb, j - 5 * nb))


def _gates_body(x_ref, w_ref, b_ref, gc_ref, gr_ref):
    z = jnp.dot(x_ref[...], w_ref[...], preferred_element_type=F32) + b_ref[...]
    sc = GATE_SOFTCAP * jnp.tanh(z / GATE_SOFTCAP)
    logf = jnp.minimum(sc, 0.0) - jnp.log1p(jnp.exp(-jnp.abs(sc)))
    rows = z.shape[0]
    row = lax.broadcasted_iota(jnp.int32, z.shape, 0)
    cum = logf
    shift = 1
    while shift < rows:
        cum = cum + jnp.where(row >= shift, pltpu.roll(cum, shift, 0), 0.0)
        shift *= 2
    lane = lax.broadcasted_iota(jnp.int32, z.shape, 1)
    gc = jnp.where(lane < N_HEADS, sc, cum)
    gc_ref[...] = gc
    gr_ref[0] = gc.T[:N_GATE_LANES, :]


def _gates(xn, w_gate, i_bias, f_bias, chunk):
    n, d = xn.shape
    bias = jnp.pad(jnp.concatenate([i_bias, f_bias]).astype(F32), (0, LANES - N_GATE_LANES))
    return pl.pallas_call(
        _gates_body,
        out_shape=(jax.ShapeDtypeStruct((n, LANES), F32),
                   jax.ShapeDtypeStruct((n // chunk, N_GATE_LANES, chunk), F32)),
        grid=(n // chunk,),
        in_specs=[pl.BlockSpec((chunk, d), lambda i: (i, 0)),
                  pl.BlockSpec((d, LANES), lambda i: (0, 0)),
                  pl.BlockSpec((1, LANES), lambda i: (0, 0))],
        out_specs=(pl.BlockSpec((chunk, LANES), lambda i: (i, 0)),
                   pl.BlockSpec((1, N_GATE_LANES, chunk), lambda i: (i, 0, 0))),
        compiler_params=_params(("parallel",), _nbytes((chunk, d), BF16) + _nbytes((d, LANES), BF16)),
        name="mlstm_gates",
    )(xn, w_gate, bias.reshape(1, LANES))


def _mlstm_body(q_ref, k_ref, v_ref, o_ref, gc_ref, gr_ref, g_ref, y_ref, c_scr, n_scr, m_scr,
                *, dqk, dv):
    @pl.when(pl.program_id(1) == 0)
    def _():
        c_scr[...] = jnp.zeros_like(c_scr)
        n_scr[...] = jnp.zeros_like(n_scr)
        m_scr[...] = jnp.zeros_like(m_scr)

    chunk = q_ref.shape[0]
    t_idx = lax.broadcasted_iota(jnp.int32, (chunk, chunk), 0)
    s_idx = lax.broadcasted_iota(jnp.int32, (chunk, chunk), 1)
    causal = s_idx <= t_idx
    gc = gc_ref[...]
    gr = gr_ref[0]
    scale = dqk ** -0.5
    for h in range(N_HEADS):
        q = q_ref[:, h * dqk:(h + 1) * dqk]
        kf = k_ref[:, h * dqk:(h + 1) * dqk].astype(F32) * scale
        v = v_ref[:, h * dv:(h + 1) * dv]
        i_col, b_col = gc[:, h:h + 1], gc[:, N_HEADS + h:N_HEADS + h + 1]
        i_row, b_row = gr[h:h + 1, :], gr[N_HEADS + h:N_HEADS + h + 1, :]
        m_prev = m_scr[h]
        d_log = jnp.where(causal, b_col - b_row + i_row, MASKED)
        inter = b_col + m_prev
        m_t = jnp.maximum(inter, jnp.max(d_log, axis=-1, keepdims=True))
        w_inter = jnp.exp(inter - m_t)
        s_qk = lax.dot_general(q, kf.astype(BF16), (((1,), (1,)), ((), ())),
                               preferred_element_type=F32) * jnp.exp(d_log - m_t)
        c_mat = c_scr[h]
        n_vec = n_scr[h]
        num = (w_inter * jnp.dot(q, c_mat.astype(BF16), preferred_element_type=F32)
               + jnp.dot(s_qk.astype(BF16), v, preferred_element_type=F32))
        den = (w_inter * jnp.sum(q.astype(F32) * n_vec, axis=-1, keepdims=True)
               + jnp.sum(s_qk, axis=-1, keepdims=True))
        hid = num / jnp.maximum(jnp.abs(den), jnp.exp(-m_t))
        m_new = m_t[chunk - 1:chunk, :]
        b_last = b_col[chunk - 1:chunk, :]
        decay = jnp.exp(b_last + m_prev - m_new)
        kw = kf * jnp.exp(b_last - b_col + i_col - m_new)
        c_scr[h] = decay * c_mat + lax.dot_general(kw.astype(BF16), v, (((0,), (0,)), ((), ())),
                                                   preferred_element_type=F32)
        n_scr[h] = decay * n_vec + jnp.sum(kw, axis=0, keepdims=True)
        m_scr[h] = m_new
        hid = hid * lax.rsqrt(jnp.mean(hid * hid, axis=-1, keepdims=True) + EPS)
        hid = hid * g_ref[:, h * dv:(h + 1) * dv]
        gate = jax.nn.sigmoid(o_ref[:, h * dv:(h + 1) * dv].astype(F32))
        y_ref[:, h * dv:(h + 1) * dv] = (gate * hid).astype(y_ref.dtype)


def _mlstm(y, gc, gr, mh_norm, batch, seq, d, chunk):
    n = batch * seq
    nc = seq // chunk
    dqk, dv = d // 2 // N_HEADS, d // N_HEADS
    rows = lambda col: (lambda b, c: (b * nc + c, col))
    blocks = (2 * _nbytes((chunk, d // 2), BF16) + 3 * _nbytes((chunk, d), BF16)
              + _nbytes((chunk, LANES), F32))
    state = _nbytes((N_HEADS, dqk, dv), F32)
    return pl.pallas_call(
        functools.partial(_mlstm_body, dqk=dqk, dv=dv),
        out_shape=jax.ShapeDtypeStruct((n, d), BF16),
        grid=(batch, nc),
        in_specs=[pl.BlockSpec((chunk, d // 2), rows(_COL_Q)),
                  pl.BlockSpec((chunk, d // 2), rows(_COL_K)),
                  pl.BlockSpec((chunk, d), rows(_COL_V)),
                  pl.BlockSpec((chunk, d), rows(_COL_O)),
                  pl.BlockSpec((chunk, LANES), rows(0)),
                  pl.BlockSpec((1, N_GATE_LANES, chunk), lambda b, c: (b * nc + c, 0, 0)),
                  pl.BlockSpec((1, d), lambda b, c: (0, 0))],
        out_specs=pl.BlockSpec((chunk, d), rows(0)),
        scratch_shapes=[pltpu.VMEM((N_HEADS, dqk, dv), F32),
                        pltpu.VMEM((N_HEADS, 1, dqk), F32),
                        pltpu.VMEM((N_HEADS, 1, 1), F32)],
        compiler_params=_params(("parallel", "arbitrary"), blocks, scratch_bytes=state,
                                temp_bytes=8 << 20),
        name="mlstm",
    )(y, y, y, y, gc, gr, mh_norm.astype(F32).reshape(1, d))


def _conv_body(bg_ref, cg_ref, vc_ref, cgp_ref, vcp_ref, w_ref, b_ref, o_ref, *, tiles_per_seq):
    u = cg_ref[...].astype(F32) * vc_ref[...].astype(F32)
    keep = (pl.program_id(0) % tiles_per_seq != 0).astype(F32)
    halo = cgp_ref[...].astype(F32) * vcp_ref[...].astype(F32) * keep
    prev1 = halo[BF16_SUBLANES - 1:BF16_SUBLANES, :]
    prev2 = halo[BF16_SUBLANES - 2:BF16_SUBLANES - 1, :]
    row = lax.broadcasted_iota(jnp.int32, u.shape, 0)
    u1 = jnp.where(row == 0, prev1, pltpu.roll(u, 1, 0))
    u2 = jnp.where(row == 0, prev2, jnp.where(row == 1, prev1, pltpu.roll(u, 2, 0)))
    w = w_ref[...]
    conv = w[0:1, :] * u2 + w[1:2, :] * u1 + w[2:3, :] * u + b_ref[...]
    o_ref[...] = (bg_ref[...].astype(F32) * conv).astype(o_ref.dtype)


def _gated_conv(y, conv_w, conv_b, seq, d):
    n = y.shape[0]
    dc = d // 2
    bm, bc = _tile(seq, 512), _tile(dc, 512)
    ncol = dc // bc
    halo_rows = bm // BF16_SUBLANES
    cur = lambda col: (lambda i, j: (i, col * ncol + j))
    prev = lambda col: (lambda i, j: (jnp.maximum(i * halo_rows - 1, 0), col * ncol + j))
    return pl.pallas_call(
        functools.partial(_conv_body, tiles_per_seq=seq // bm),
        out_shape=jax.ShapeDtypeStruct((n, dc), BF16),
        grid=(n // bm, ncol),
        in_specs=[pl.BlockSpec((bm, bc), cur(_COL_BG)),
                  pl.BlockSpec((bm, bc), cur(_COL_CG)),
                  pl.BlockSpec((bm, bc), cur(_COL_VC)),
                  pl.BlockSpec((BF16_SUBLANES, bc), prev(_COL_CG)),
                  pl.BlockSpec((BF16_SUBLANES, bc), prev(_COL_VC)),
                  pl.BlockSpec((conv_w.shape[0], bc), lambda i, j: (0, j)),
                  pl.BlockSpec((1, bc), lambda i, j: (0, j))],
        out_specs=pl.BlockSpec((bm, bc), lambda i, j: (i, j)),
        compiler_params=_params(("parallel", "parallel"), 4 * _nbytes((bm, bc), BF16),
                                temp_bytes=8 * _nbytes((bm, bc), F32)),
        name="gated_conv",
    )(y, y, y, y, y, conv_w.astype(F32), conv_b.astype(F32).reshape(1, dc))


def _merge_body(ya_ref, yb_ref, pc_ref, pm_ref, ga_ref, gb_ref, o_ref):
    a = jnp.dot(ya_ref[...], pc_ref[...], preferred_element_type=F32)
    b = jnp.dot(yb_ref[...], pm_ref[...], preferred_element_type=F32)
    merged = (jax.nn.sigmoid(ga_ref[...].astype(F32)) * a
              + jax.nn.sigmoid(gb_ref[...].astype(F32)) * b)
    o_ref[...] = merged.astype(o_ref.dtype)


def _merge(y_a, y_b, p_conv, p_mlstm, yg, d, bm_pref=1024, bn_pref=512):
    n = y_a.shape[0]
    bm, bn = _tile(n, bm_pref), _tile(d, bn_pref)
    ncol = d // bn
    blocks = (_nbytes((bm, d // 2), BF16) + _nbytes((bm, d), BF16) + _nbytes((d // 2, bn), BF16)
              + _nbytes((d, bn), BF16) + 3 * _nbytes((bm, bn), BF16))
    return pl.pallas_call(
        _merge_body,
        out_shape=jax.ShapeDtypeStruct((n, d), BF16),
        grid=(n // bm, ncol),
        in_specs=[pl.BlockSpec((bm, d // 2), lambda i, j: (i, 0)),
                  pl.BlockSpec((bm, d), lambda i, j: (i, 0)),
                  pl.BlockSpec((d // 2, bn), lambda i, j: (0, j)),
                  pl.BlockSpec((d, bn), lambda i, j: (0, j)),
                  pl.BlockSpec((bm, bn), lambda i, j: (i, _COL_GA * ncol + j)),
                  pl.BlockSpec((bm, bn), lambda i, j: (i, _COL_GB * ncol + j))],
        out_specs=pl.BlockSpec((bm, bn), lambda i, j: (i, j)),
        compiler_params=_params(("parallel", "parallel"), blocks, temp_bytes=3 * _nbytes((bm, bn), F32)),
        name="merge",
    )(y_a, y_b, p_conv, p_mlstm, yg, yg)


def _hybrid_mixer(x, ln, w_in, layer, conv_w, conv_b, i_bias, f_bias, mh_norm, p_conv, p_mlstm, w_out,
                  batch, seq):
    n, d = x.shape
    dc = d // 2
    chunk = _tile(seq, MLSTM_CHUNK)
    n_main = _N_MAIN_UNITS * dc
    gate0 = n_main + N_GATE_LANES
    w_gate = jnp.pad(w_in[layer, :, n_main:gate0], ((0, 0), (0, LANES - N_GATE_LANES))).astype(BF16)
    w_merge = w_in[layer, :, gate0:].astype(BF16)[None]
    xn = _rmsnorm(x, ln, BF16)
    bn = _tile(dc, 512)
    y = _matmul(xn, w_in, layer, n_main, BF16, out_block=_main_out_block(dc // bn), bn_pref=bn)
    yg = _matmul(xn, w_merge, 0, 2 * d, BF16)
    gc, gr = _gates(xn, w_gate, i_bias, f_bias, chunk)
    y_b = _mlstm(y, gc, gr, mh_norm, batch, seq, d, chunk)
    y_a = _gated_conv(y, conv_w, conv_b, seq, d)
    merged = _merge(y_a, y_b, p_conv.astype(BF16), p_mlstm.astype(BF16), yg, d)
    return _matmul_residual(merged, w_out.astype(BF16), x)


_R_E1, _R_E2, _R_RANK1, _R_RANK2, _R_W1, _R_W2 = 0, 1, 2, 3, 4, 5


def _route_body(lg_ref, r_ref, cnt_ref, carry, *, n_exp):
    @pl.when(pl.program_id(0) == 0)
    def _():
        carry[...] = jnp.zeros_like(carry)

    lg = lg_ref[...]
    bt = lg.shape[0]
    lane_i = lax.broadcasted_iota(jnp.int32, lg.shape, 1)
    lane = lane_i.astype(F32)
    x1 = jnp.where(lane_i < n_exp, lg, MASKED)
    m1 = jnp.max(x1, axis=-1, keepdims=True)
    e1 = jnp.min(jnp.where(x1 == m1, lane, float(LANES)), axis=-1, keepdims=True)
    x2 = jnp.where(lane == e1, MASKED, x1)
    m2 = jnp.max(x2, axis=-1, keepdims=True)
    e2 = jnp.min(jnp.where(x2 == m2, lane, float(LANES)), axis=-1, keepdims=True)
    ex = jnp.exp(m2 - m1)
    w1 = 1.0 / (1.0 + ex)
    w2 = ex / (1.0 + ex)
    hit1, hit2 = lane == e1, lane == e2
    onehot = (hit1 | hit2).astype(F32)
    t_idx = lax.broadcasted_iota(jnp.int32, (bt, bt), 0)
    s_idx = lax.broadcasted_iota(jnp.int32, (bt, bt), 1)
    earlier = (s_idx < t_idx).astype(BF16)
    rank = jnp.dot(earlier, onehot.astype(BF16), preferred_element_type=F32) + carry[...]
    rank1 = jnp.sum(jnp.where(hit1, rank, 0.0), axis=-1, keepdims=True)
    rank2 = jnp.sum(jnp.where(hit2, rank, 0.0), axis=-1, keepdims=True)
    carry[...] += jnp.sum(onehot, axis=0, keepdims=True)
    rec = jnp.zeros_like(lg)
    for idx, val in ((_R_E1, e1), (_R_E2, e2), (_R_RANK1, rank1), (_R_RANK2, rank2),
                     (_R_W1, w1), (_R_W2, w2)):
        rec = jnp.where(lane_i == idx, val, rec)
    r_ref[...] = rec
    cnt_ref[...] = jnp.broadcast_to(carry[...], cnt_ref.shape)


def _route(logits, n_exp):
    n = logits.shape[0]
    bt = _tile(n, 512)
    return pl.pallas_call(
        functools.partial(_route_body, n_exp=n_exp),
        out_shape=(jax.ShapeDtypeStruct((n, LANES), F32), jax.ShapeDtypeStruct((8, LANES), F32)),
        grid=(n // bt,),
        in_specs=[pl.BlockSpec((bt, LANES), lambda i: (i, 0))],
        out_specs=(pl.BlockSpec((bt, LANES), lambda i: (i, 0)),
                   pl.BlockSpec((8, LANES), lambda i: (0, 0))),
        scratch_shapes=[pltpu.VMEM((1, LANES), F32)],
        compiler_params=_params(("arbitrary",), 2 * _nbytes((bt, LANES), F32),
                                temp_bytes=_nbytes((bt, bt), F32)),
        name="moe_route",
    )(logits)


def _row_copy(src_hbm, src_row, dst_ref, dst_row, sem):
    return pltpu.make_async_copy(src_hbm.at[pl.ds(src_row, 1)], dst_ref.at[pl.ds(dst_row, 1)], sem)


def _invert_slots_body(pos1_ref, pos2_ref, tok_ref, *, n, n_slots):
    def clear(s, carry):
        tok_ref[s] = 0
        return carry

    def put(t, carry):
        tok_ref[pos1_ref[t]] = t
        tok_ref[pos2_ref[t]] = t
        return carry

    lax.fori_loop(0, n_slots, clear, 0, unroll=8)
    lax.fori_loop(0, n, put, 0, unroll=8)


def _invert_slots(pos1, pos2, n_slots):
    smem = pl.BlockSpec(memory_space=pltpu.SMEM)
    return pl.pallas_call(
        functools.partial(_invert_slots_body, n=pos1.shape[0], n_slots=n_slots),
        out_shape=jax.ShapeDtypeStruct((n_slots,), jnp.int32),
        in_specs=[smem, smem],
        out_specs=smem,
        name="moe_invert",
    )(pos1, pos2)


def _gather_rows_body(tok_ref, h_hbm, o_ref, buf, sems, *, bt, n_steps):
    i = pl.program_id(0)
    slot = i % 2

    def issue(step, to_slot):
        def one(r, carry):
            _row_copy(h_hbm, tok_ref[step * bt + r], buf.at[to_slot], r, sems.at[to_slot]).start()
            return carry
        lax.fori_loop(0, bt, one, 0, unroll=8)

    @pl.when(i == 0)
    def _():
        issue(0, 0)

    @pl.when(i + 1 < n_steps)
    def _():
        issue(i + 1, 1 - slot)

    def drain(r, carry):
        _row_copy(h_hbm, 0, buf.at[slot], 0, sems.at[slot]).wait()
        return carry

    lax.fori_loop(0, bt, drain, 0, unroll=8)
    o_ref[...] = buf[slot].astype(o_ref.dtype)


def _gather_rows(h, tok, n_slots):
    d = h.shape[1]
    bt = _tile(n_slots, 256)
    n_steps = n_slots // bt
    return pl.pallas_call(
        functools.partial(_gather_rows_body, bt=bt, n_steps=n_steps),
        out_shape=jax.ShapeDtypeStruct((n_slots, d), BF16),
        grid_spec=pltpu.PrefetchScalarGridSpec(
            num_scalar_prefetch=1,
            grid=(n_steps,),
            in_specs=[pl.BlockSpec(memory_space=pl.ANY)],
            out_specs=pl.BlockSpec((bt, d), lambda i, tok: (i, 0)),
            scratch_shapes=[pltpu.VMEM((2, bt, d), F32), pltpu.SemaphoreType.DMA((2,))]),
        compiler_params=_params(("arbitrary",), _nbytes((bt, d), BF16),
                                scratch_bytes=_nbytes((2, bt, d), F32), temp_bytes=_nbytes((bt, d), F32)),
        name="moe_gather",
    )(tok, h)


def _tile_state(te_ref, nt_ref):
    p = pl.program_id(1)
    live = p < nt_ref[0]
    fresh = (p == 0) | (te_ref[p] != te_ref[jnp.maximum(p - 1, 0)])
    return live, live & fresh


def _moe_up_body(te_ref, nt_ref, x_ref, wg_ref, wu_ref, o_ref, wg_scr, wu_scr):
    live, fresh = _tile_state(te_ref, nt_ref)

    @pl.when(fresh)
    def _():
        wg_scr[...] = wg_ref[...].astype(BF16)
        wu_scr[...] = wu_ref[...].astype(BF16)

    @pl.when(live)
    def _():
        o_ref[...] = _swiglu(x_ref[...], wg_scr[...], wu_scr[...]).astype(o_ref.dtype)

    @pl.when(jnp.logical_not(live))
    def _():
        o_ref[...] = jnp.zeros_like(o_ref)


def _moe_up(xs, w4, layer, tile_expert, n_tiles, bm, bn_pref=512):
    n_slots, d = xs.shape
    f = w4.shape[3] // 2
    bn = _tile(f, bn_pref)
    nj = f // bn
    blocks = _nbytes((bm, d), xs.dtype) + 2 * _nbytes((d, bn), w4.dtype) + _nbytes((bm, bn), BF16)
    return pl.pallas_call(
        _moe_up_body,
        out_shape=jax.ShapeDtypeStruct((n_slots, f), BF16),
        grid_spec=pltpu.PrefetchScalarGridSpec(
            num_scalar_prefetch=2,
            grid=(nj, n_slots // bm),
            in_specs=[pl.BlockSpec((bm, d), lambda j, p, te, nt: (jnp.minimum(p, nt[0] - 1), 0)),
                      pl.BlockSpec((None, None, d, bn), lambda j, p, te, nt: (layer, te[p], 0, j)),
                      pl.BlockSpec((None, None, d, bn), lambda j, p, te, nt: (layer, te[p], 0, nj + j))],
            out_specs=pl.BlockSpec((bm, bn), lambda j, p, te, nt: (p, j)),
            scratch_shapes=[pltpu.VMEM((d, bn), BF16), pltpu.VMEM((d, bn), BF16)]),
        compiler_params=_params(("arbitrary", "arbitrary"), blocks, scratch_bytes=2 * _nbytes((d, bn), BF16),
                                temp_bytes=3 * _nbytes((bm, bn), F32)),
        name="moe_up",
    )(tile_expert, n_tiles, xs, w4, w4)


def _moe_down_body(te_ref, nt_ref, a_ref, w_ref, o_ref, w_scr):
    live, fresh = _tile_state(te_ref, nt_ref)

    @pl.when(fresh)
    def _():
        w_scr[...] = w_ref[...].astype(BF16)

    @pl.when(live)
    def _():
        o_ref[...] = jnp.dot(a_ref[...], w_scr[...], preferred_element_type=F32)

    @pl.when(jnp.logical_not(live))
    def _():
        o_ref[...] = jnp.zeros_like(o_ref)


def _moe_down(act, w4, layer, tile_expert, n_tiles, bm, bn_pref=512):
    n_slots, f = act.shape
    d = w4.shape[3]
    bn = _tile(d, bn_pref)
    blocks = _nbytes((bm, f), BF16) + _nbytes((f, bn), w4.dtype) + _nbytes((bm, bn), F32)
    return pl.pallas_call(
        _moe_down_body,
        out_shape=jax.ShapeDtypeStruct((n_slots, d), F32),
        grid_spec=pltpu.PrefetchScalarGridSpec(
            num_scalar_prefetch=2,
            grid=(d // bn, n_slots // bm),
            in_specs=[pl.BlockSpec((bm, f), lambda j, p, te, nt: (jnp.minimum(p, nt[0] - 1), 0)),
                      pl.BlockSpec((None, None, f, bn), lambda j, p, te, nt: (layer, te[p], 0, j))],
            out_specs=pl.BlockSpec((bm, bn), lambda j, p, te, nt: (p, j)),
            scratch_shapes=[pltpu.VMEM((f, bn), BF16)]),
        compiler_params=_params(("arbitrary", "arbitrary"), blocks, scratch_bytes=_nbytes((f, bn), BF16),
                                temp_bytes=_nbytes((bm, bn), F32)),
        name="moe_down",
    )(tile_expert, n_tiles, act, w4)


def _combine_body(pos1_ref, pos2_ref, r_ref, x_ref, ys_hbm, *rest, bt, n_steps, final_norm):
    if final_norm:
        g_ref, o_ref, buf, sems = rest
    else:
        o_ref, buf, sems = rest
    i = pl.program_id(0)
    slot = i % 2

    def issue(step, to_slot):
        def one(r, carry):
            t = step * bt + r
            _row_copy(ys_hbm, pos1_ref[t], buf.at[to_slot, 0], r, sems.at[to_slot]).start()
            _row_copy(ys_hbm, pos2_ref[t], buf.at[to_slot, 1], r, sems.at[to_slot]).start()
            return carry
        lax.fori_loop(0, bt, one, 0, unroll=8)

    @pl.when(i == 0)
    def _():
        issue(0, 0)

    @pl.when(i + 1 < n_steps)
    def _():
        issue(i + 1, 1 - slot)

    def drain(r, carry):
        _row_copy(ys_hbm, 0, buf.at[slot, 0], 0, sems.at[slot]).wait()
        _row_copy(ys_hbm, 0, buf.at[slot, 1], 0, sems.at[slot]).wait()
        return carry

    lax.fori_loop(0, bt, drain, 0, unroll=8)
    w1 = r_ref[:, _R_W1:_R_W1 + 1]
    w2 = r_ref[:, _R_W2:_R_W2 + 1]
    out = x_ref[...] + (w1 * buf[slot, 0] + w2 * buf[slot, 1])
    if final_norm:
        out = (out * lax.rsqrt(jnp.mean(out * out, axis=-1, keepdims=True) + EPS)) * g_ref[...]
    o_ref[...] = out


def _combine(x, ys, route, pos1, pos2, final_gain=None):
    n, d = x.shape
    bt = _tile(n, 256)
    n_steps = n // bt
    final_norm = final_gain is not None
    in_specs = [pl.BlockSpec((bt, LANES), lambda i, p1, p2: (i, 0)),
                pl.BlockSpec((bt, d), lambda i, p1, p2: (i, 0)),
                pl.BlockSpec(memory_space=pl.ANY)]
    args = [pos1, pos2, route, x, ys]
    if final_norm:
        in_specs.append(pl.BlockSpec((1, d), lambda i, p1, p2: (0, 0)))
        args.append(final_gain.astype(F32).reshape(1, d))
    return pl.pallas_call(
        functools.partial(_combine_body, bt=bt, n_steps=n_steps, final_norm=final_norm),
        out_shape=jax.ShapeDtypeStruct((n, d), F32),
        grid_spec=pltpu.PrefetchScalarGridSpec(
            num_scalar_prefetch=2,
            grid=(n_steps,),
            in_specs=in_specs,
            out_specs=pl.BlockSpec((bt, d), lambda i, p1, p2: (i, 0)),
            scratch_shapes=[pltpu.VMEM((2, TOP_K, bt, d), F32), pltpu.SemaphoreType.DMA((2,))]),
        compiler_params=_params(("arbitrary",), 2 * _nbytes((bt, d), F32),
                                scratch_bytes=_nbytes((2, TOP_K, bt, d), F32),
                                temp_bytes=2 * _nbytes((bt, d), F32)),
        name="moe_combine",
    )(*args)


def _moe(x, ln, router_w, router_b, w_gate_up, w_down, layer, final_gain, bm_pref=512):
    n, d = x.shape
    n_exp = router_w.shape[1]
    bm = _tile(n * TOP_K, bm_pref)
    n_tiles_max = (n * TOP_K) // bm + n_exp
    h_f32, logits = _rmsnorm_router(x, ln, router_w, router_b)
    route, counts = _route(logits, n_exp)
    counts = counts[0, :n_exp].astype(jnp.int32)
    sizes = (counts + bm - 1) // bm * bm
    ends = jnp.cumsum(sizes)
    starts = ends - sizes
    e1, e2 = route[:, _R_E1].astype(jnp.int32), route[:, _R_E2].astype(jnp.int32)
    pos1 = starts[e1] + route[:, _R_RANK1].astype(jnp.int32)
    pos2 = starts[e2] + route[:, _R_RANK2].astype(jnp.int32)
    n_tiles = (ends[-1] // bm).astype(jnp.int32).reshape(1)
    tile_start = jnp.arange(n_tiles_max, dtype=jnp.int32) * bm
    tile_expert = jnp.sum(tile_start[:, None] >= ends[None, :], axis=1).astype(jnp.int32)
    tile_expert = jnp.minimum(tile_expert, tile_expert[n_tiles[0] - 1])

    n_slots = n_tiles_max * bm
    xs = _gather_rows(h_f32, _invert_slots(pos1, pos2, n_slots), n_slots)
    act = _moe_up(xs, w_gate_up, layer, tile_expert, n_tiles, bm)
    ys = _moe_down(act, w_down, layer, tile_expert, n_tiles, bm)
    return _combine(x, ys, route, pos1, pos2, final_gain)


def kernel(x, ln_mix, ln_ffn, w_in, conv_w, conv_b, i_bias, f_bias, mh_norm, p_conv, p_mlstm, w_out,
           ffn_w_gate_up, ffn_w_down, router_w, router_b, exp_w_gate_up, exp_w_down, final_norm):
    batch, seq, d = x.shape
    depth = ln_mix.shape[0]
    xf = x.reshape(batch * seq, d)
    normed = False
    for layer in range(depth):
        xf = _hybrid_mixer(xf, ln_mix[layer], w_in, layer, conv_w[layer], conv_b[layer],
                           i_bias[layer], f_bias[layer], mh_norm[layer], p_conv[layer],
                           p_mlstm[layer], w_out[layer], batch, seq)
        j = layer // 2
        if layer % 2 == 0:
            h = _rmsnorm(xf, ln_ffn[layer], BF16)
            act = _swiglu_up(h, ffn_w_gate_up, j)
            xf = _matmul_residual_ktiled(act, ffn_w_down[j].astype(BF16), xf)
        else:
            normed = layer == depth - 1
            xf = _moe(xf, ln_ffn[layer], router_w[j], router_b[j], exp_w_gate_up, exp_w_down, j,
                      final_norm if normed else None)
    if not normed:
        xf = _rmsnorm(xf, final_norm, x.dtype)
    return xf.reshape(batch, seq, d)
```

```python
import functools

import jax
import jax.numpy as jnp
from jax import lax
from jax.experimental import pallas as pl
from jax.experimental.pallas import tpu as pltpu

F32 = jnp.float32
BF16 = jnp.bfloat16

EPS = 1e-6
GATE_SOFTCAP = 15.0
N_HEADS = 8
TOP_K = 2

V7X_VMEM_BYTES = 64 * 2**20
LANES = 128
BF16_SUBLANES = 16
MASKED = -1e30

MLSTM_CHUNK = 256
N_GATE_LANES = 2 * N_HEADS


def _tile(dim, pref):
    t = min(dim, pref)
    while dim % t:
        t //= 2
    return t


def _params(semantics, block_bytes, scratch_bytes=0, temp_bytes=0):
    need = 2 * block_bytes + scratch_bytes + temp_bytes + (4 << 20)
    return pltpu.CompilerParams(
        dimension_semantics=semantics,
        vmem_limit_bytes=int(min(max(need, 32 << 20), V7X_VMEM_BYTES - (6 << 20))))


def _nbytes(shape, dtype):
    n = 1
    for s in shape:
        n *= s
    return n * jnp.dtype(dtype).itemsize


def _rmsnorm_body(x_ref, g_ref, o_ref):
    x = x_ref[...]
    inv = lax.rsqrt(jnp.mean(x * x, axis=-1, keepdims=True) + EPS)
    o_ref[...] = ((x * inv) * g_ref[...]).astype(o_ref.dtype)


def _rmsnorm(x, g, out_dtype):
    n, d = x.shape
    bm = _tile(n, 256)
    return pl.pallas_call(
        _rmsnorm_body,
        out_shape=jax.ShapeDtypeStruct((n, d), out_dtype),
        grid=(n // bm,),
        in_specs=[pl.BlockSpec((bm, d), lambda i: (i, 0)),
                  pl.BlockSpec((1, d), lambda i: (0, 0))],
        out_specs=pl.BlockSpec((bm, d), lambda i: (i, 0)),
        compiler_params=_params(("parallel",), _nbytes((bm, d), F32) * 2),
        name="rmsnorm",
    )(x, g.reshape(1, d))


def _rmsnorm_router_body(x_ref, g_ref, whi_ref, wlo_ref, rb_ref, hf_ref, lg_ref):
    x = x_ref[...]
    inv = lax.rsqrt(jnp.mean(x * x, axis=-1, keepdims=True) + EPS)
    h = (x * inv) * g_ref[...]
    h_hi = h.astype(BF16)
    hf_ref[...] = h
    h_lo = (h - h_hi.astype(F32)).astype(BF16)
    lg = (jnp.dot(h_hi, whi_ref[...], preferred_element_type=F32)
          + jnp.dot(h_lo, whi_ref[...], preferred_element_type=F32)
          + jnp.dot(h_hi, wlo_ref[...], preferred_element_type=F32))
    lg_ref[...] = lg + rb_ref[...]


def _rmsnorm_router(x, g, router_w, router_b):
    n, d = x.shape
    n_exp = router_w.shape[1]
    bm = _tile(n, 256)
    w = jnp.pad(router_w.astype(F32), ((0, 0), (0, LANES - n_exp)))
    w_hi = w.astype(BF16)
    w_lo = (w - w_hi.astype(F32)).astype(BF16)
    rb = jnp.pad(router_b.astype(F32), (0, LANES - n_exp)).reshape(1, LANES)
    row = lambda i: (i, 0)
    fixed = lambda i: (0, 0)
    return pl.pallas_call(
        _rmsnorm_router_body,
        out_shape=(jax.ShapeDtypeStruct((n, d), F32), jax.ShapeDtypeStruct((n, LANES), F32)),
        grid=(n // bm,),
        in_specs=[pl.BlockSpec((bm, d), row), pl.BlockSpec((1, d), fixed),
                  pl.BlockSpec((d, LANES), fixed), pl.BlockSpec((d, LANES), fixed),
                  pl.BlockSpec((1, LANES), fixed)],
        out_specs=(pl.BlockSpec((bm, d), row), pl.BlockSpec((bm, LANES), row)),
        compiler_params=_params(("parallel",), _nbytes((bm, d), F32) * 2 + _nbytes((d, LANES), BF16) * 2),
        name="rmsnorm_router",
    )(x, g.reshape(1, d), w_hi, w_lo, rb)


def _refresh_weights(copies, stages, w_scrs, n_blocks):
    j = pl.program_id(0)

    @pl.when(pl.program_id(1) == 0)
    def _():
        @pl.when(j == 0)
        def _():
            for copy in copies:
                copy(0).start()

        for copy, stage, w_scr in zip(copies, stages, w_scrs):
            copy(j).wait()
            w_scr[...] = stage[...].astype(BF16)

        @pl.when(j + 1 < n_blocks)
        def _():
            for copy in copies:
                copy(j + 1).start()


def _in_proj_body(x_ref, wt_hbm, o_ref, stage, w_scr, sem, *, layer, bn, n_blocks, n_main_blocks):
    def copy(j):
        row0 = j * bn + jnp.where(j >= n_main_blocks, N_GATE_LANES, 0)
        return pltpu.make_async_copy(wt_hbm.at[layer, pl.ds(row0, bn)], stage, sem)

    _refresh_weights([copy], [stage], [w_scr], n_blocks)
    o_ref[...] = lax.dot_general(x_ref[...], w_scr[...], (((1,), (1,)), ((), ())),
                                 preferred_element_type=F32).astype(o_ref.dtype)


def _in_proj(x, wt, layer, d, bm_pref=1024, bn_pref=1024):
    m, k = x.shape
    dc = d // 2
    bm, bn = _tile(m, bm_pref), _tile(dc, bn_pref)
    nb = dc // bn
    n_units = _N_MAIN_UNITS + 4
    n_blocks = n_units * nb
    out_block = lambda j: jnp.where(j < 3 * nb, j + 10 * nb, jnp.where(j < 5 * nb, j + 5 * nb, j - 5 * nb))
    blocks = _nbytes((bm, k), x.dtype) + _nbytes((bm, bn), BF16)
    scratch = _nbytes((bn, k), F32) + _nbytes((bn, k), BF16)
    return pl.pallas_call(
        functools.partial(_in_proj_body, layer=layer, bn=bn, n_blocks=n_blocks,
                          n_main_blocks=_N_MAIN_UNITS * nb),
        out_shape=jax.ShapeDtypeStruct((m, n_units * dc), BF16),
        grid=(n_blocks, m // bm),
        in_specs=[pl.BlockSpec((bm, k), lambda j, i: (i, 0)),
                  pl.BlockSpec(memory_space=pl.ANY)],
        out_specs=pl.BlockSpec((bm, bn), lambda j, i: (i, out_block(j))),
        scratch_shapes=[pltpu.VMEM((bn, k), F32), pltpu.VMEM((bn, k), BF16), pltpu.SemaphoreType.DMA(())],
        compiler_params=_params(("arbitrary", "arbitrary"), blocks, scratch_bytes=scratch,
                                temp_bytes=_nbytes((bm, bn), F32)),
        name="in_proj",
    )(x, wt)


def _matmul_residual_body(a_ref, w_ref, x_ref, o_ref):
    o_ref[...] = x_ref[...] + jnp.dot(a_ref[...], w_ref[...], preferred_element_type=F32)


def _matmul_residual(a, w, x, bm_pref=1024, bn_pref=512):
    m, k = a.shape
    n = w.shape[1]
    bm, bn = _tile(m, bm_pref), _tile(n, bn_pref)
    blocks = _nbytes((bm, k), a.dtype) + _nbytes((k, bn), w.dtype) + 2 * _nbytes((bm, bn), F32)
    return pl.pallas_call(
        _matmul_residual_body,
        out_shape=jax.ShapeDtypeStruct((m, n), F32),
        grid=(m // bm, n // bn),
        in_specs=[pl.BlockSpec((bm, k), lambda i, j: (i, 0)),
                  pl.BlockSpec((k, bn), lambda i, j: (0, j)),
                  pl.BlockSpec((bm, bn), lambda i, j: (i, j))],
        out_specs=pl.BlockSpec((bm, bn), lambda i, j: (i, j)),
        compiler_params=_params(("parallel", "parallel"), blocks, temp_bytes=_nbytes((bm, bn), F32)),
        name="out_proj",
    )(a, w, x)


def _matmul_residual_ktiled_body(a_ref, w_ref, x_ref, o_ref):
    part = jnp.dot(a_ref[...], w_ref[...], preferred_element_type=F32)

    @pl.when(pl.program_id(2) == 0)
    def _():
        o_ref[...] = x_ref[...] + part

    @pl.when(pl.program_id(2) != 0)
    def _():
        o_ref[...] += part


def _matmul_residual_ktiled(a, w, x, bm_pref=1024, bn_pref=1024, bk_pref=2048):
    m, k = a.shape
    n = w.shape[1]
    bm, bn, bk = _tile(m, bm_pref), _tile(n, bn_pref), _tile(k, bk_pref)
    blocks = _nbytes((bm, bk), a.dtype) + _nbytes((bk, bn), w.dtype) + 2 * _nbytes((bm, bn), F32)
    return pl.pallas_call(
        _matmul_residual_ktiled_body,
        out_shape=jax.ShapeDtypeStruct((m, n), F32),
        grid=(m // bm, n // bn, k // bk),
        in_specs=[pl.BlockSpec((bm, bk), lambda i, j, kk: (i, kk)),
                  pl.BlockSpec((bk, bn), lambda i, j, kk: (kk, j)),
                  pl.BlockSpec((bm, bn), lambda i, j, kk: (i, j))],
        out_specs=pl.BlockSpec((bm, bn), lambda i, j, kk: (i, j)),
        compiler_params=_params(("parallel", "parallel", "arbitrary"), blocks,
                                temp_bytes=_nbytes((bm, bn), F32)),
        name="ffn_down",
    )(a, w, x)


def _swiglu(h, wg, wu):
    g = jnp.dot(h, wg, preferred_element_type=F32)
    u = jnp.dot(h, wu, preferred_element_type=F32)
    return g * jax.nn.sigmoid(g) * u


def _swiglu_up_body(h_ref, w_hbm, o_ref, g_stage, u_stage, wg_scr, wu_scr, sems, *, layer, bn, n_blocks):
    def copy_gate(j):
        return pltpu.make_async_copy(w_hbm.at[layer, :, pl.ds(j * bn, bn)], g_stage, sems.at[0])

    def copy_up(j):
        return pltpu.make_async_copy(w_hbm.at[layer, :, pl.ds((n_blocks + j) * bn, bn)], u_stage, sems.at[1])

    _refresh_weights([copy_gate, copy_up], [g_stage, u_stage], [wg_scr, wu_scr], n_blocks)
    o_ref[...] = _swiglu(h_ref[...], wg_scr[...], wu_scr[...]).astype(o_ref.dtype)


def _swiglu_up(h, w3, layer, bm_pref=1024, bn_pref=512):
    m, k = h.shape
    f = w3.shape[2] // 2
    bm, bn = _tile(m, bm_pref), _tile(f, bn_pref)
    n_blocks = f // bn
    blocks = _nbytes((bm, k), h.dtype) + _nbytes((bm, bn), BF16)
    scratch = 2 * (_nbytes((k, bn), F32) + _nbytes((k, bn), BF16))
    return pl.pallas_call(
        functools.partial(_swiglu_up_body, layer=layer, bn=bn, n_blocks=n_blocks),
        out_shape=jax.ShapeDtypeStruct((m, f), BF16),
        grid=(n_blocks, m // bm),
        in_specs=[pl.BlockSpec((bm, k), lambda j, i: (i, 0)),
                  pl.BlockSpec(memory_space=pl.ANY)],
        out_specs=pl.BlockSpec((bm, bn), lambda j, i: (i, j)),
        scratch_shapes=[pltpu.VMEM((k, bn), F32), pltpu.VMEM((k, bn), F32),
                        pltpu.VMEM((k, bn), BF16), pltpu.VMEM((k, bn), BF16),
                        pltpu.SemaphoreType.DMA((2,))],
        compiler_params=_params(("arbitrary", "arbitrary"), blocks, scratch_bytes=scratch,
                                temp_bytes=3 * _nbytes((bm, bn), F32)),
        name="ffn_up",
    )(h, w3)


_COL_V, _COL_O, _COL_GA, _COL_GB = 0, 1, 2, 3
_COL_Q, _COL_K, _COL_BG, _COL_CG, _COL_VC = 8, 9, 10, 11, 12
_N_MAIN_UNITS = 9


def _gates_body(x_ref, wt_hbm, b_ref, gc_ref, gr_ref, stage, w_scr, sem, *, layer, row0):
    @pl.when(pl.program_id(0) == 0)
    def _():
        stage[...] = jnp.zeros_like(stage)
        rows = pltpu.make_async_copy(wt_hbm.at[layer, pl.ds(row0, N_GATE_LANES)],
                                     stage.at[pl.ds(0, N_GATE_LANES)], sem)
        rows.start()
        rows.wait()
        w_scr[...] = stage[...].astype(BF16)

    z = lax.dot_general(x_ref[...], w_scr[...], (((1,), (1,)), ((), ())),
                        preferred_element_type=F32) + b_ref[...]
    sc = GATE_SOFTCAP * jnp.tanh(z / GATE_SOFTCAP)
    logf = jnp.minimum(sc, 0.0) - jnp.log1p(jnp.exp(-jnp.abs(sc)))
    rows = z.shape[0]
    row = lax.broadcasted_iota(jnp.int32, z.shape, 0)
    cum = logf
    shift = 1
    while shift < rows:
        cum = cum + jnp.where(row >= shift, pltpu.roll(cum, shift, 0), 0.0)
        shift *= 2
    lane = lax.broadcasted_iota(jnp.int32, z.shape, 1)
    gc = jnp.where(lane < N_HEADS, sc, cum)
    gc_ref[...] = gc
    gr_ref[0] = gc.T[:N_GATE_LANES, :]


def _gates(xn, wt, layer, i_bias, f_bias, chunk):
    n, d = xn.shape
    bias = jnp.pad(jnp.concatenate([i_bias, f_bias]).astype(F32), (0, LANES - N_GATE_LANES))
    return pl.pallas_call(
        functools.partial(_gates_body, layer=layer, row0=_N_MAIN_UNITS * (d // 2)),
        out_shape=(jax.ShapeDtypeStruct((n, LANES), F32),
                   jax.ShapeDtypeStruct((n // chunk, N_GATE_LANES, chunk), F32)),
        grid=(n // chunk,),
        in_specs=[pl.BlockSpec((chunk, d), lambda i: (i, 0)),
                  pl.BlockSpec(memory_space=pl.ANY),
                  pl.BlockSpec((1, LANES), lambda i: (0, 0))],
        out_specs=(pl.BlockSpec((chunk, LANES), lambda i: (i, 0)),
                   pl.BlockSpec((1, N_GATE_LANES, chunk), lambda i: (i, 0, 0))),
        scratch_shapes=[pltpu.VMEM((LANES, d), F32), pltpu.VMEM((LANES, d), BF16),
                        pltpu.SemaphoreType.DMA(())],
        compiler_params=_params(("arbitrary",), _nbytes((chunk, d), BF16),
                                scratch_bytes=_nbytes((LANES, d), F32) + _nbytes((LANES, d), BF16)),
        name="mlstm_gates",
    )(xn, wt, bias.reshape(1, LANES))


def _mlstm_body(q_ref, k_ref, v_ref, o_ref, gc_ref, gr_ref, g_ref, y_ref, c_scr, n_scr, m_scr,
                *, dqk, dv):
    @pl.when(pl.program_id(1) == 0)
    def _():
        c_scr[...] = jnp.zeros_like(c_scr)
        n_scr[...] = jnp.zeros_like(n_scr)
        m_scr[...] = jnp.zeros_like(m_scr)

    chunk = q_ref.shape[0]
    t_idx = lax.broadcasted_iota(jnp.int32, (chunk, chunk), 0)
    s_idx = lax.broadcasted_iota(jnp.int32, (chunk, chunk), 1)
    causal = s_idx <= t_idx
    gc = gc_ref[...]
    gr = gr_ref[0]
    scale = dqk ** -0.5
    for h in range(N_HEADS):
        q = q_ref[:, h * dqk:(h + 1) * dqk]
        kf = k_ref[:, h * dqk:(h + 1) * dqk].astype(F32) * scale
        v = v_ref[:, h * dv:(h + 1) * dv]
        i_col, b_col = gc[:, h:h + 1], gc[:, N_HEADS + h:N_HEADS + h + 1]
        i_row, b_row = gr[h:h + 1, :], gr[N_HEADS + h:N_HEADS + h + 1, :]
        m_prev = m_scr[h]
        d_log = jnp.where(causal, b_col - b_row + i_row, MASKED)
        inter = b_col + m_prev
        m_t = jnp.maximum(inter, jnp.max(d_log, axis=-1, keepdims=True))
        w_inter = jnp.exp(inter - m_t)
        s_qk = lax.dot_general(q, kf.astype(BF16), (((1,), (1,)), ((), ())),
                               preferred_element_type=F32) * jnp.exp(d_log - m_t)
        c_mat = c_scr[h]
        n_vec = n_scr[h]
        num = (w_inter * jnp.dot(q, c_mat.astype(BF16), preferred_element_type=F32)
               + jnp.dot(s_qk.astype(BF16), v, preferred_element_type=F32))
        den = (w_inter * jnp.sum(q.astype(F32) * n_vec, axis=-1, keepdims=True)
               + jnp.sum(s_qk, axis=-1, keepdims=True))
        hid = num / jnp.maximum(jnp.abs(den), jnp.exp(-m_t))
        m_new = m_t[chunk - 1:chunk, :]
        b_last = b_col[chunk - 1:chunk, :]
        decay = jnp.exp(b_last + m_prev - m_new)
        kw = kf * jnp.exp(b_last - b_col + i_col - m_new)
        c_scr[h] = decay * c_mat + lax.dot_general(kw.astype(BF16), v, (((0,), (0,)), ((), ())),
                                                   preferred_element_type=F32)
        n_scr[h] = decay * n_vec + jnp.sum(kw, axis=0, keepdims=True)
        m_scr[h] = m_new
        hid = hid * lax.rsqrt(jnp.mean(hid * hid, axis=-1, keepdims=True) + EPS)
        hid = hid * g_ref[:, h * dv:(h + 1) * dv]
        gate = jax.nn.sigmoid(o_ref[:, h * dv:(h + 1) * dv].astype(F32))
        y_ref[:, h * dv:(h + 1) * dv] = (gate * hid).astype(y_ref.dtype)


def _mlstm(y, gc, gr, mh_norm, batch, seq, d, chunk):
    n = batch * seq
    nc = seq // chunk
    dqk, dv = d // 2 // N_HEADS, d // N_HEADS
    rows = lambda col: (lambda b, c: (b * nc + c, col))
    blocks = (2 * _nbytes((chunk, d // 2), BF16) + 3 * _nbytes((chunk, d), BF16)
              + _nbytes((chunk, LANES), F32))
    state = _nbytes((N_HEADS, dqk, dv), F32)
    return pl.pallas_call(
        functools.partial(_mlstm_body, dqk=dqk, dv=dv),
        out_shape=jax.ShapeDtypeStruct((n, d), BF16),
        grid=(batch, nc),
        in_specs=[pl.BlockSpec((chunk, d // 2), rows(_COL_Q)),
                  pl.BlockSpec((chunk, d // 2), rows(_COL_K)),
                  pl.BlockSpec((chunk, d), rows(_COL_V)),
                  pl.BlockSpec((chunk, d), rows(_COL_O)),
                  pl.BlockSpec((chunk, LANES), rows(0)),
                  pl.BlockSpec((1, N_GATE_LANES, chunk), lambda b, c: (b * nc + c, 0, 0)),
                  pl.BlockSpec((1, d), lambda b, c: (0, 0))],
        out_specs=pl.BlockSpec((chunk, d), rows(0)),
        scratch_shapes=[pltpu.VMEM((N_HEADS, dqk, dv), F32),
                        pltpu.VMEM((N_HEADS, 1, dqk), F32),
                        pltpu.VMEM((N_HEADS, 1, 1), F32)],
        compiler_params=_params(("parallel", "arbitrary"), blocks, scratch_bytes=state,
                                temp_bytes=8 << 20),
        name="mlstm",
    )(y, y, y, y, gc, gr, mh_norm.astype(F32).reshape(1, d))


def _conv_body(bg_ref, cg_ref, vc_ref, cgp_ref, vcp_ref, w_ref, b_ref, o_ref, *, tiles_per_seq):
    u = cg_ref[...].astype(F32) * vc_ref[...].astype(F32)
    keep = (pl.program_id(0) % tiles_per_seq != 0).astype(F32)
    halo = cgp_ref[...].astype(F32) * vcp_ref[...].astype(F32) * keep
    prev1 = halo[BF16_SUBLANES - 1:BF16_SUBLANES, :]
    prev2 = halo[BF16_SUBLANES - 2:BF16_SUBLANES - 1, :]
    row = lax.broadcasted_iota(jnp.int32, u.shape, 0)
    u1 = jnp.where(row == 0, prev1, pltpu.roll(u, 1, 0))
    u2 = jnp.where(row == 0, prev2, jnp.where(row == 1, prev1, pltpu.roll(u, 2, 0)))
    w = w_ref[...]
    conv = w[0:1, :] * u2 + w[1:2, :] * u1 + w[2:3, :] * u + b_ref[...]
    o_ref[...] = (bg_ref[...].astype(F32) * conv).astype(o_ref.dtype)


def _gated_conv(y, conv_w, conv_b, seq, d):
    n = y.shape[0]
    dc = d // 2
    bm, bc = _tile(seq, 512), _tile(dc, 512)
    ncol = dc // bc
    halo_rows = bm // BF16_SUBLANES
    cur = lambda col: (lambda i, j: (i, col * ncol + j))
    prev = lambda col: (lambda i, j: (jnp.maximum(i * halo_rows - 1, 0), col * ncol + j))
    return pl.pallas_call(
        functools.partial(_conv_body, tiles_per_seq=seq // bm),
        out_shape=jax.ShapeDtypeStruct((n, dc), BF16),
        grid=(n // bm, ncol),
        in_specs=[pl.BlockSpec((bm, bc), cur(_COL_BG)),
                  pl.BlockSpec((bm, bc), cur(_COL_CG)),
                  pl.BlockSpec((bm, bc), cur(_COL_VC)),
                  pl.BlockSpec((BF16_SUBLANES, bc), prev(_COL_CG)),
                  pl.BlockSpec((BF16_SUBLANES, bc), prev(_COL_VC)),
                  pl.BlockSpec((conv_w.shape[0], bc), lambda i, j: (0, j)),
                  pl.BlockSpec((1, bc), lambda i, j: (0, j))],
        out_specs=pl.BlockSpec((bm, bc), lambda i, j: (i, j)),
        compiler_params=_params(("parallel", "parallel"), 4 * _nbytes((bm, bc), BF16),
                                temp_bytes=8 * _nbytes((bm, bc), F32)),
        name="gated_conv",
    )(y, y, y, y, y, conv_w.astype(F32), conv_b.astype(F32).reshape(1, dc))


def _merge_body(ya_ref, yb_ref, pc_ref, pm_ref, ga_ref, gb_ref, o_ref):
    a = jnp.dot(ya_ref[...], pc_ref[...], preferred_element_type=F32)
    b = jnp.dot(yb_ref[...], pm_ref[...], preferred_element_type=F32)
    merged = (jax.nn.sigmoid(ga_ref[...].astype(F32)) * a
              + jax.nn.sigmoid(gb_ref[...].astype(F32)) * b)
    o_ref[...] = merged.astype(o_ref.dtype)


def _merge(y_a, y_b, p_conv, p_mlstm, y, d, bm_pref=1024, bn_pref=512):
    n = y_a.shape[0]
    bm, bn = _tile(n, bm_pref), _tile(d, bn_pref)
    ncol = d // bn
    blocks = (_nbytes((bm, d // 2), BF16) + _nbytes((bm, d), BF16) + _nbytes((d // 2, bn), BF16)
              + _nbytes((d, bn), BF16) + 3 * _nbytes((bm, bn), BF16))
    return pl.pallas_call(
        _merge_body,
        out_shape=jax.ShapeDtypeStruct((n, d), BF16),
        grid=(n // bm, ncol),
        in_specs=[pl.BlockSpec((bm, d // 2), lambda i, j: (i, 0)),
                  pl.BlockSpec((bm, d), lambda i, j: (i, 0)),
                  pl.BlockSpec((d // 2, bn), lambda i, j: (0, j)),
                  pl.BlockSpec((d, bn), lambda i, j: (0, j)),
                  pl.BlockSpec((bm, bn), lambda i, j: (i, _COL_GA * ncol + j)),
                  pl.BlockSpec((bm, bn), lambda i, j: (i, _COL_GB * ncol + j))],
        out_specs=pl.BlockSpec((bm, bn), lambda i, j: (i, j)),
        compiler_params=_params(("parallel", "parallel"), blocks, temp_bytes=3 * _nbytes((bm, bn), F32)),
        name="merge",
    )(y_a, y_b, p_conv, p_mlstm, y, y)


def _hybrid_mixer(x, ln, wt_in, layer, conv_w, conv_b, i_bias, f_bias, mh_norm, p_conv, p_mlstm, w_out,
                  batch, seq):
    n, d = x.shape
    chunk = _tile(seq, MLSTM_CHUNK)
    xn = _rmsnorm(x, ln, BF16)
    y = _in_proj(xn, wt_in, layer, d)
    gc, gr = _gates(xn, wt_in, layer, i_bias, f_bias, chunk)
    y_b = _mlstm(y, gc, gr, mh_norm, batch, seq, d, chunk)
    y_a = _gated_conv(y, conv_w, conv_b, seq, d)
    merged = _merge(y_a, y_b, p_conv.astype(BF16), p_mlstm.astype(BF16), y, d)
    return _matmul_residual(merged, w_out.astype(BF16), x)


_R_E1, _R_E2, _R_RANK1, _R_RANK2, _R_W1, _R_W2 = 0, 1, 2, 3, 4, 5


def _route_body(lg_ref, r_ref, cnt_ref, carry, *, n_exp):
    @pl.when(pl.program_id(0) == 0)
    def _():
        carry[...] = jnp.zeros_like(carry)

    lg = lg_ref[...]
    bt = lg.shape[0]
    lane_i = lax.broadcasted_iota(jnp.int32, lg.shape, 1)
    lane = lane_i.astype(F32)
    x1 = jnp.where(lane_i < n_exp, lg, MASKED)
    m1 = jnp.max(x1, axis=-1, keepdims=True)
    e1 = jnp.min(jnp.where(x1 == m1, lane, float(LANES)), axis=-1, keepdims=True)
    x2 = jnp.where(lane == e1, MASKED, x1)
    m2 = jnp.max(x2, axis=-1, keepdims=True)
    e2 = jnp.min(jnp.where(x2 == m2, lane, float(LANES)), axis=-1, keepdims=True)
    ex = jnp.exp(m2 - m1)
    w1 = 1.0 / (1.0 + ex)
    w2 = ex / (1.0 + ex)
    hit1, hit2 = lane == e1, lane == e2
    onehot = (hit1 | hit2).astype(F32)
    t_idx = lax.broadcasted_iota(jnp.int32, (bt, bt), 0)
    s_idx = lax.broadcasted_iota(jnp.int32, (bt, bt), 1)
    earlier = (s_idx < t_idx).astype(BF16)
    rank = jnp.dot(earlier, onehot.astype(BF16), preferred_element_type=F32) + carry[...]
    rank1 = jnp.sum(jnp.where(hit1, rank, 0.0), axis=-1, keepdims=True)
    rank2 = jnp.sum(jnp.where(hit2, rank, 0.0), axis=-1, keepdims=True)
    carry[...] += jnp.sum(onehot, axis=0, keepdims=True)
    rec = jnp.zeros_like(lg)
    for idx, val in ((_R_E1, e1), (_R_E2, e2), (_R_RANK1, rank1), (_R_RANK2, rank2),
                     (_R_W1, w1), (_R_W2, w2)):
        rec = jnp.where(lane_i == idx, val, rec)
    r_ref[...] = rec
    cnt_ref[...] = jnp.broadcast_to(carry[...], cnt_ref.shape)


def _route(logits, n_exp):
    n = logits.shape[0]
    bt = _tile(n, 512)
    return pl.pallas_call(
        functools.partial(_route_body, n_exp=n_exp),
        out_shape=(jax.ShapeDtypeStruct((n, LANES), F32), jax.ShapeDtypeStruct((8, LANES), F32)),
        grid=(n // bt,),
        in_specs=[pl.BlockSpec((bt, LANES), lambda i: (i, 0))],
        out_specs=(pl.BlockSpec((bt, LANES), lambda i: (i, 0)),
                   pl.BlockSpec((8, LANES), lambda i: (0, 0))),
        scratch_shapes=[pltpu.VMEM((1, LANES), F32)],
        compiler_params=_params(("arbitrary",), 2 * _nbytes((bt, LANES), F32),
                                temp_bytes=_nbytes((bt, bt), F32)),
        name="moe_route",
    )(logits)


def _row_copy(src_hbm, src_row, dst_ref, dst_row, sem):
    return pltpu.make_async_copy(src_hbm.at[pl.ds(src_row, 1)], dst_ref.at[pl.ds(dst_row, 1)], sem)


def _invert_slots_body(pos1_ref, pos2_ref, tok_ref, *, n, n_slots):
    def clear(s, carry):
        tok_ref[s] = 0
        return carry

    def put(t, carry):
        tok_ref[pos1_ref[t]] = t
        tok_ref[pos2_ref[t]] = t
        return carry

    lax.fori_loop(0, n_slots, clear, 0, unroll=8)
    lax.fori_loop(0, n, put, 0, unroll=8)


def _invert_slots(pos1, pos2, n_slots):
    smem = pl.BlockSpec(memory_space=pltpu.SMEM)
    return pl.pallas_call(
        functools.partial(_invert_slots_body, n=pos1.shape[0], n_slots=n_slots),
        out_shape=jax.ShapeDtypeStruct((n_slots,), jnp.int32),
        in_specs=[smem, smem],
        out_specs=smem,
        name="moe_invert",
    )(pos1, pos2)


def _gather_rows_body(tok_ref, h_hbm, o_ref, buf, sems, *, bt, n_steps):
    i = pl.program_id(0)
    slot = i % 2

    def issue(step, to_slot):
        def one(r, carry):
            _row_copy(h_hbm, tok_ref[step * bt + r], buf.at[to_slot], r, sems.at[to_slot]).start()
            return carry
        lax.fori_loop(0, bt, one, 0, unroll=8)

    @pl.when(i == 0)
    def _():
        issue(0, 0)

    @pl.when(i + 1 < n_steps)
    def _():
        issue(i + 1, 1 - slot)

    def drain(r, carry):
        _row_copy(h_hbm, 0, buf.at[slot], 0, sems.at[slot]).wait()
        return carry

    lax.fori_loop(0, bt, drain, 0, unroll=8)
    o_ref[...] = buf[slot].astype(o_ref.dtype)


def _gather_rows(h, tok, n_slots):
    d = h.shape[1]
    bt = _tile(n_slots, 256)
    n_steps = n_slots // bt
    return pl.pallas_call(
        functools.partial(_gather_rows_body, bt=bt, n_steps=n_steps),
        out_shape=jax.ShapeDtypeStruct((n_slots, d), BF16),
        grid_spec=pltpu.PrefetchScalarGridSpec(
            num_scalar_prefetch=1,
            grid=(n_steps,),
            in_specs=[pl.BlockSpec(memory_space=pl.ANY)],
            out_specs=pl.BlockSpec((bt, d), lambda i, tok: (i, 0)),
            scratch_shapes=[pltpu.VMEM((2, bt, d), F32), pltpu.SemaphoreType.DMA((2,))]),
        compiler_params=_params(("arbitrary",), _nbytes((bt, d), BF16),
                                scratch_bytes=_nbytes((2, bt, d), F32), temp_bytes=_nbytes((bt, d), F32)),
        name="moe_gather",
    )(tok, h)


def _tile_state(te_ref, nt_ref):
    p = pl.program_id(1)
    live = p < nt_ref[0]
    fresh = (p == 0) | (te_ref[p] != te_ref[jnp.maximum(p - 1, 0)])
    return live, live & fresh


def _moe_up_body(te_ref, nt_ref, x_ref, wg_ref, wu_ref, o_ref, wg_scr, wu_scr):
    live, fresh = _tile_state(te_ref, nt_ref)

    @pl.when(fresh)
    def _():
        wg_scr[...] = wg_ref[...].astype(BF16)
        wu_scr[...] = wu_ref[...].astype(BF16)

    @pl.when(live)
    def _():
        o_ref[...] = _swiglu(x_ref[...], wg_scr[...], wu_scr[...]).astype(o_ref.dtype)

    @pl.when(jnp.logical_not(live))
    def _():
        o_ref[...] = jnp.zeros_like(o_ref)


def _moe_up(xs, w4, layer, tile_expert, n_tiles, bm, bn_pref=512):
    n_slots, d = xs.shape
    f = w4.shape[3] // 2
    bn = _tile(f, bn_pref)
    nj = f // bn
    blocks = _nbytes((bm, d), xs.dtype) + 2 * _nbytes((d, bn), w4.dtype) + _nbytes((bm, bn), BF16)
    return pl.pallas_call(
        _moe_up_body,
        out_shape=jax.ShapeDtypeStruct((n_slots, f), BF16),
        grid_spec=pltpu.PrefetchScalarGridSpec(
            num_scalar_prefetch=2,
            grid=(nj, n_slots // bm),
            in_specs=[pl.BlockSpec((bm, d), lambda j, p, te, nt: (jnp.minimum(p, nt[0] - 1), 0)),
                      pl.BlockSpec((None, None, d, bn), lambda j, p, te, nt: (layer, te[p], 0, j)),
                      pl.BlockSpec((None, None, d, bn), lambda j, p, te, nt: (layer, te[p], 0, nj + j))],
            out_specs=pl.BlockSpec((bm, bn), lambda j, p, te, nt: (p, j)),
            scratch_shapes=[pltpu.VMEM((d, bn), BF16), pltpu.VMEM((d, bn), BF16)]),
        compiler_params=_params(("arbitrary", "arbitrary"), blocks, scratch_bytes=2 * _nbytes((d, bn), BF16),
                                temp_bytes=3 * _nbytes((bm, bn), F32)),
        name="moe_up",
    )(tile_expert, n_tiles, xs, w4, w4)


def _moe_down_body(te_ref, nt_ref, a_ref, w_ref, o_ref, w_scr):
    live, fresh = _tile_state(te_ref, nt_ref)

    @pl.when(fresh)
    def _():
        w_scr[...] = w_ref[...].astype(BF16)

    @pl.when(live)
    def _():
        o_ref[...] = jnp.dot(a_ref[...], w_scr[...], preferred_element_type=F32)

    @pl.when(jnp.logical_not(live))
    def _():
        o_ref[...] = jnp.zeros_like(o_ref)


def _moe_down(act, w4, layer, tile_expert, n_tiles, bm, bn_pref=512):
    n_slots, f = act.shape
    d = w4.shape[3]
    bn = _tile(d, bn_pref)
    blocks = _nbytes((bm, f), BF16) + _nbytes((f, bn), w4.dtype) + _nbytes((bm, bn), F32)
    return pl.pallas_call(
        _moe_down_body,
        out_shape=jax.ShapeDtypeStruct((n_slots, d), F32),
        grid_spec=pltpu.PrefetchScalarGridSpec(
            num_scalar_prefetch=2,
            grid=(d // bn, n_slots // bm),
            in_specs=[pl.BlockSpec((bm, f), lambda j, p, te, nt: (jnp.minimum(p, nt[0] - 1), 0)),
                      pl.BlockSpec((None, None, f, bn), lambda j, p, te, nt: (layer, te[p], 0, j))],
            out_specs=pl.BlockSpec((bm, bn), lambda j, p, te, nt: (p, j)),
            scratch_shapes=[pltpu.VMEM((f, bn), BF16)]),
        compiler_params=_params(("arbitrary", "arbitrary"), blocks, scratch_bytes=_nbytes((f, bn), BF16),
                                temp_bytes=_nbytes((bm, bn), F32)),
        name="moe_down",
    )(tile_expert, n_tiles, act, w4)


def _combine_body(pos1_ref, pos2_ref, r_ref, x_ref, ys_hbm, *rest, bt, n_steps, final_norm):
    if final_norm:
        g_ref, o_ref, buf, sems = rest
    else:
        o_ref, buf, sems = rest
    i = pl.program_id(0)
    slot = i % 2

    def issue(step, to_slot):
        def one(r, carry):
            t = step * bt + r
            _row_copy(ys_hbm, pos1_ref[t], buf.at[to_slot, 0], r, sems.at[to_slot]).start()
            _row_copy(ys_hbm, pos2_ref[t], buf.at[to_slot, 1], r, sems.at[to_slot]).start()
            return carry
        lax.fori_loop(0, bt, one, 0, unroll=8)

    @pl.when(i == 0)
    def _():
        issue(0, 0)

    @pl.when(i + 1 < n_steps)
    def _():
        issue(i + 1, 1 - slot)

    def drain(r, carry):
        _row_copy(ys_hbm, 0, buf.at[slot, 0], 0, sems.at[slot]).wait()
        _row_copy(ys_hbm, 0, buf.at[slot, 1], 0, sems.at[slot]).wait()
        return carry

    lax.fori_loop(0, bt, drain, 0, unroll=8)
    w1 = r_ref[:, _R_W1:_R_W1 + 1]
    w2 = r_ref[:, _R_W2:_R_W2 + 1]
    out = x_ref[...] + (w1 * buf[slot, 0] + w2 * buf[slot, 1])
    if final_norm:
        out = (out * lax.rsqrt(jnp.mean(out * out, axis=-1, keepdims=True) + EPS)) * g_ref[...]
    o_ref[...] = out


def _combine(x, ys, route, pos1, pos2, final_gain=None):
    n, d = x.shape
    bt = _tile(n, 256)
    n_steps = n // bt
    final_norm = final_gain is not None
    in_specs = [pl.BlockSpec((bt, LANES), lambda i, p1, p2: (i, 0)),
                pl.BlockSpec((bt, d), lambda i, p1, p2: (i, 0)),
                pl.BlockSpec(memory_space=pl.ANY)]
    args = [pos1, pos2, route, x, ys]
    if final_norm:
        in_specs.append(pl.BlockSpec((1, d), lambda i, p1, p2: (0, 0)))
        args.append(final_gain.astype(F32).reshape(1, d))
    return pl.pallas_call(
        functools.partial(_combine_body, bt=bt, n_steps=n_steps, final_norm=final_norm),
        out_shape=jax.ShapeDtypeStruct((n, d), F32),
        grid_spec=pltpu.PrefetchScalarGridSpec(
            num_scalar_prefetch=2,
            grid=(n_steps,),
            in_specs=in_specs,
            out_specs=pl.BlockSpec((bt, d), lambda i, p1, p2: (i, 0)),
            scratch_shapes=[pltpu.VMEM((2, TOP_K, bt, d), F32), pltpu.SemaphoreType.DMA((2,))]),
        compiler_params=_params(("arbitrary",), 2 * _nbytes((bt, d), F32),
                                scratch_bytes=_nbytes((2, TOP_K, bt, d), F32),
                                temp_bytes=2 * _nbytes((bt, d), F32)),
        name="moe_combine",
    )(*args)


def _moe(x, ln, router_w, router_b, w_gate_up, w_down, layer, final_gain, bm_pref=512):
    n, d = x.shape
    n_exp = router_w.shape[1]
    bm = _tile(n * TOP_K, bm_pref)
    n_tiles_max = (n * TOP_K) // bm + n_exp
    h_f32, logits = _rmsnorm_router(x, ln, router_w, router_b)
    route, counts = _route(logits, n_exp)
    counts = counts[0, :n_exp].astype(jnp.int32)
    sizes = (counts + bm - 1) // bm * bm
    ends = jnp.cumsum(sizes)
    starts = ends - sizes
    e1, e2 = route[:, _R_E1].astype(jnp.int32), route[:, _R_E2].astype(jnp.int32)
    pos1 = starts[e1] + route[:, _R_RANK1].astype(jnp.int32)
    pos2 = starts[e2] + route[:, _R_RANK2].astype(jnp.int32)
    n_tiles = (ends[-1] // bm).astype(jnp.int32).reshape(1)
    tile_start = jnp.arange(n_tiles_max, dtype=jnp.int32) * bm
    tile_expert = jnp.sum(tile_start[:, None] >= ends[None, :], axis=1).astype(jnp.int32)
    tile_expert = jnp.minimum(tile_expert, tile_expert[n_tiles[0] - 1])

    n_slots = n_tiles_max * bm
    xs = _gather_rows(h_f32, _invert_slots(pos1, pos2, n_slots), n_slots)
    act = _moe_up(xs, w_gate_up, layer, tile_expert, n_tiles, bm)
    ys = _moe_down(act, w_down, layer, tile_expert, n_tiles, bm)
    return _combine(x, ys, route, pos1, pos2, final_gain)


def kernel(x, ln_mix, ln_ffn, w_in, conv_w, conv_b, i_bias, f_bias, mh_norm, p_conv, p_mlstm, w_out,
           ffn_w_gate_up, ffn_w_down, router_w, router_b, exp_w_gate_up, exp_w_down, final_norm):
    batch, seq, d = x.shape
    depth = ln_mix.shape[0]
    xf = x.reshape(batch * seq, d)
    wt_in = jnp.transpose(w_in, (0, 2, 1))
    normed = False
    for layer in range(depth):
        xf = _hybrid_mixer(xf, ln_mix[layer], wt_in, layer, conv_w[layer], conv_b[layer],
                           i_bias[layer], f_bias[layer], mh_norm[layer], p_conv[layer],
                           p_mlstm[layer], w_out[layer], batch, seq)
        j = layer // 2
        if layer % 2 == 0:
            h = _rmsnorm(xf, ln_ffn[layer], BF16)
            act = _swiglu_up(h, ffn_w_gate_up, j)
            xf = _matmul_residual_ktiled(act, ffn_w_down[j].astype(BF16), xf)
        else:
            normed = layer == depth - 1
            xf = _moe(xf, ln_ffn[layer], router_w[j], router_b[j], exp_w_gate_up, exp_w_down, j,
                      final_norm if normed else None)
    if not normed:
        xf = _rmsnorm(xf, final_norm, x.dtype)
    return xf.reshape(batch, seq, d)
```

```python
import functools

import jax
import jax.numpy as jnp
from jax import lax
from jax.experimental import pallas as pl
from jax.experimental.pallas import tpu as pltpu

F32 = jnp.float32
BF16 = jnp.bfloat16

EPS = 1e-6
GATE_SOFTCAP = 15.0
N_HEADS = 8
TOP_K = 2

V7X_VMEM_BYTES = 64 * 2**20
LANES = 128
BF16_SUBLANES = 16
MASKED = -1e30

MLSTM_CHUNK = 256
N_GATE_LANES = 2 * N_HEADS


def _tile(dim, pref):
    if dim <= pref:
        return dim
    t = pref
    while dim % t:
        t -= LANES
    return t


def _params(semantics, block_bytes, scratch_bytes=0, temp_bytes=0):
    need = 2 * block_bytes + scratch_bytes + temp_bytes + (4 << 20)
    return pltpu.CompilerParams(
        dimension_semantics=semantics,
        vmem_limit_bytes=int(min(max(need, 32 << 20), V7X_VMEM_BYTES - (6 << 20))))


def _nbytes(shape, dtype):
    n = 1
    for s in shape:
        n *= s
    return n * jnp.dtype(dtype).itemsize


def _rmsnorm_body(x_ref, g_ref, o_ref):
    x = x_ref[...]
    inv = lax.rsqrt(jnp.mean(x * x, axis=-1, keepdims=True) + EPS)
    o_ref[...] = ((x * inv) * g_ref[...]).astype(o_ref.dtype)


def _rmsnorm(x, g, out_dtype):
    n, d = x.shape
    bm = _tile(n, 256)
    return pl.pallas_call(
        _rmsnorm_body,
        out_shape=jax.ShapeDtypeStruct((n, d), out_dtype),
        grid=(n // bm,),
        in_specs=[pl.BlockSpec((bm, d), lambda i: (i, 0)),
                  pl.BlockSpec((1, d), lambda i: (0, 0))],
        out_specs=pl.BlockSpec((bm, d), lambda i: (i, 0)),
        compiler_params=_params(("parallel",), _nbytes((bm, d), F32) * 2),
        name="rmsnorm",
    )(x, g.reshape(1, d))


def _rmsnorm_router_body(x_ref, g_ref, whi_ref, wlo_ref, rb_ref, hf_ref, lg_ref):
    x = x_ref[...]
    inv = lax.rsqrt(jnp.mean(x * x, axis=-1, keepdims=True) + EPS)
    h = (x * inv) * g_ref[...]
    h_hi = h.astype(BF16)
    hf_ref[...] = h
    h_lo = (h - h_hi.astype(F32)).astype(BF16)
    lg = (jnp.dot(h_hi, whi_ref[...], preferred_element_type=F32)
          + jnp.dot(h_lo, whi_ref[...], preferred_element_type=F32)
          + jnp.dot(h_hi, wlo_ref[...], preferred_element_type=F32))
    lg_ref[...] = lg + rb_ref[...]


def _rmsnorm_router(x, g, router_w, router_b):
    n, d = x.shape
    n_exp = router_w.shape[1]
    bm = _tile(n, 256)
    w = jnp.pad(router_w.astype(F32), ((0, 0), (0, LANES - n_exp)))
    w_hi = w.astype(BF16)
    w_lo = (w - w_hi.astype(F32)).astype(BF16)
    rb = jnp.pad(router_b.astype(F32), (0, LANES - n_exp)).reshape(1, LANES)
    row = lambda i: (i, 0)
    fixed = lambda i: (0, 0)
    return pl.pallas_call(
        _rmsnorm_router_body,
        out_shape=(jax.ShapeDtypeStruct((n, d), F32), jax.ShapeDtypeStruct((n, LANES), F32)),
        grid=(n // bm,),
        in_specs=[pl.BlockSpec((bm, d), row), pl.BlockSpec((1, d), fixed),
                  pl.BlockSpec((d, LANES), fixed), pl.BlockSpec((d, LANES), fixed),
                  pl.BlockSpec((1, LANES), fixed)],
        out_specs=(pl.BlockSpec((bm, d), row), pl.BlockSpec((bm, LANES), row)),
        compiler_params=_params(("parallel",), _nbytes((bm, d), F32) * 2 + _nbytes((d, LANES), BF16) * 2),
        name="rmsnorm_router",
    )(x, g.reshape(1, d), w_hi, w_lo, rb)


def _refresh_weights(copies, stages, w_scrs, n_blocks):
    j = pl.program_id(0)

    @pl.when(pl.program_id(1) == 0)
    def _():
        @pl.when(j == 0)
        def _():
            for copy in copies:
                copy(0).start()

        for copy, stage, w_scr in zip(copies, stages, w_scrs):
            copy(j).wait()
            w_scr[...] = stage[...].astype(BF16)

        @pl.when(j + 1 < n_blocks)
        def _():
            for copy in copies:
                copy(j + 1).start()


def _in_proj_body(x_ref, wt_hbm, o_ref, stage, w_scr, sem, *, layer, bn, n_blocks, n_main_blocks):
    def copy(j):
        row0 = j * bn + jnp.where(j >= n_main_blocks, N_GATE_LANES, 0)
        return pltpu.make_async_copy(wt_hbm.at[layer, pl.ds(row0, bn)], stage, sem)

    _refresh_weights([copy], [stage], [w_scr], n_blocks)
    o_ref[...] = lax.dot_general(x_ref[...], w_scr[...], (((1,), (1,)), ((), ())),
                                 preferred_element_type=F32).astype(o_ref.dtype)


def _in_proj(x, wt, layer, d, bm_pref=1024, bn_pref=1024):
    m, k = x.shape
    dc = d // 2
    bm, bn = _tile(m, bm_pref), _tile(dc, bn_pref)
    nb = dc // bn
    n_units = _N_MAIN_UNITS + 4
    n_blocks = n_units * nb
    out_block = lambda j: jnp.where(j < 3 * nb, j + 10 * nb, jnp.where(j < 5 * nb, j + 5 * nb, j - 5 * nb))
    blocks = _nbytes((bm, k), x.dtype) + _nbytes((bm, bn), BF16)
    scratch = _nbytes((bn, k), F32) + _nbytes((bn, k), BF16)
    return pl.pallas_call(
        functools.partial(_in_proj_body, layer=layer, bn=bn, n_blocks=n_blocks,
                          n_main_blocks=_N_MAIN_UNITS * nb),
        out_shape=jax.ShapeDtypeStruct((m, n_units * dc), BF16),
        grid=(n_blocks, m // bm),
        in_specs=[pl.BlockSpec((bm, k), lambda j, i: (i, 0)),
                  pl.BlockSpec(memory_space=pl.ANY)],
        out_specs=pl.BlockSpec((bm, bn), lambda j, i: (i, out_block(j))),
        scratch_shapes=[pltpu.VMEM((bn, k), F32), pltpu.VMEM((bn, k), BF16), pltpu.SemaphoreType.DMA(())],
        compiler_params=_params(("arbitrary", "arbitrary"), blocks, scratch_bytes=scratch,
                                temp_bytes=_nbytes((bm, bn), F32)),
        name="in_proj",
    )(x, wt)


def _matmul_residual_body(a_ref, w_hbm, x_ref, o_ref, stage, w_scr, sem, *, layer, bn, n_blocks):
    def copy(j):
        return pltpu.make_async_copy(w_hbm.at[layer, :, pl.ds(j * bn, bn)], stage, sem)

    _refresh_weights([copy], [stage], [w_scr], n_blocks)
    o_ref[...] = x_ref[...] + jnp.dot(a_ref[...], w_scr[...], preferred_element_type=F32)


def _matmul_residual(a, w3, layer, x, bm_pref=1024, bn_pref=512):
    m, k = a.shape
    n = w3.shape[2]
    bm, bn = _tile(m, bm_pref), _tile(n, bn_pref)
    n_blocks = n // bn
    blocks = _nbytes((bm, k), a.dtype) + 2 * _nbytes((bm, bn), F32)
    scratch = _nbytes((k, bn), F32) + _nbytes((k, bn), BF16)
    return pl.pallas_call(
        functools.partial(_matmul_residual_body, layer=layer, bn=bn, n_blocks=n_blocks),
        out_shape=jax.ShapeDtypeStruct((m, n), F32),
        grid=(n_blocks, m // bm),
        in_specs=[pl.BlockSpec((bm, k), lambda j, i: (i, 0)),
                  pl.BlockSpec(memory_space=pl.ANY),
                  pl.BlockSpec((bm, bn), lambda j, i: (i, j))],
        out_specs=pl.BlockSpec((bm, bn), lambda j, i: (i, j)),
        scratch_shapes=[pltpu.VMEM((k, bn), F32), pltpu.VMEM((k, bn), BF16), pltpu.SemaphoreType.DMA(())],
        compiler_params=_params(("arbitrary", "arbitrary"), blocks, scratch_bytes=scratch,
                                temp_bytes=_nbytes((bm, bn), F32)),
        name="out_proj",
    )(a, w3, x)


def _matmul_residual_ktiled_body(a_ref, w_ref, x_ref, o_ref):
    @pl.when(pl.program_id(2) == 0)
    def _():
        o_ref[...] = x_ref[...]

    o_ref[...] += jnp.dot(a_ref[...], w_ref[...], preferred_element_type=F32)


def _matmul_residual_ktiled(a, w, x, bm_pref=1024, bn_pref=1024, bk_pref=3584):
    m, k = a.shape
    n = w.shape[1]
    bm, bn, bk = _tile(m, bm_pref), _tile(n, bn_pref), _tile(k, bk_pref)
    blocks = _nbytes((bm, bk), a.dtype) + _nbytes((bk, bn), w.dtype) + 2 * _nbytes((bm, bn), F32)
    return pl.pallas_call(
        _matmul_residual_ktiled_body,
        out_shape=jax.ShapeDtypeStruct((m, n), F32),
        grid=(m // bm, n // bn, k // bk),
        in_specs=[pl.BlockSpec((bm, bk), lambda i, j, kk: (i, kk)),
                  pl.BlockSpec((bk, bn), lambda i, j, kk: (kk, j)),
                  pl.BlockSpec((bm, bn), lambda i, j, kk: (i, j))],
        out_specs=pl.BlockSpec((bm, bn), lambda i, j, kk: (i, j)),
        compiler_params=_params(("parallel", "parallel", "arbitrary"), blocks,
                                temp_bytes=_nbytes((bm, bn), F32)),
        name="ffn_down",
    )(a, w, x)


def _swiglu(h, wg, wu):
    g = jnp.dot(h, wg, preferred_element_type=F32)
    u = jnp.dot(h, wu, preferred_element_type=F32)
    return g * jax.nn.sigmoid(g) * u


def _swiglu_up_body(h_ref, w_hbm, o_ref, g_stage, u_stage, wg_scr, wu_scr, sems, *, layer, bn, n_blocks):
    def copy_gate(j):
        return pltpu.make_async_copy(w_hbm.at[layer, :, pl.ds(j * bn, bn)], g_stage, sems.at[0])

    def copy_up(j):
        return pltpu.make_async_copy(w_hbm.at[layer, :, pl.ds((n_blocks + j) * bn, bn)], u_stage, sems.at[1])

    _refresh_weights([copy_gate, copy_up], [g_stage, u_stage], [wg_scr, wu_scr], n_blocks)
    o_ref[...] = _swiglu(h_ref[...], wg_scr[...], wu_scr[...]).astype(o_ref.dtype)


def _swiglu_up(h, w3, layer, bm_pref=1024, bn_pref=512):
    m, k = h.shape
    f = w3.shape[2] // 2
    bm, bn = _tile(m, bm_pref), _tile(f, bn_pref)
    n_blocks = f // bn
    blocks = _nbytes((bm, k), h.dtype) + _nbytes((bm, bn), BF16)
    scratch = 2 * (_nbytes((k, bn), F32) + _nbytes((k, bn), BF16))
    return pl.pallas_call(
        functools.partial(_swiglu_up_body, layer=layer, bn=bn, n_blocks=n_blocks),
        out_shape=jax.ShapeDtypeStruct((m, f), BF16),
        grid=(n_blocks, m // bm),
        in_specs=[pl.BlockSpec((bm, k), lambda j, i: (i, 0)),
                  pl.BlockSpec(memory_space=pl.ANY)],
        out_specs=pl.BlockSpec((bm, bn), lambda j, i: (i, j)),
        scratch_shapes=[pltpu.VMEM((k, bn), F32), pltpu.VMEM((k, bn), F32),
                        pltpu.VMEM((k, bn), BF16), pltpu.VMEM((k, bn), BF16),
                        pltpu.SemaphoreType.DMA((2,))],
        compiler_params=_params(("arbitrary", "arbitrary"), blocks, scratch_bytes=scratch,
                                temp_bytes=3 * _nbytes((bm, bn), F32)),
        name="ffn_up",
    )(h, w3)


_COL_V, _COL_O, _COL_GA, _COL_GB = 0, 1, 2, 3
_COL_Q, _COL_K, _COL_BG, _COL_CG, _COL_VC = 8, 9, 10, 11, 12
_N_MAIN_UNITS = 9


def _gates_body(x_ref, wt_hbm, b_ref, gc_ref, gr_ref, stage, w_scr, sem, *, layer, row0):
    @pl.when(pl.program_id(0) == 0)
    def _():
        stage[...] = jnp.zeros_like(stage)
        rows = pltpu.make_async_copy(wt_hbm.at[layer, pl.ds(row0, N_GATE_LANES)],
                                     stage.at[pl.ds(0, N_GATE_LANES)], sem)
        rows.start()
        rows.wait()
        w_scr[...] = stage[...].astype(BF16)

    z = lax.dot_general(x_ref[...], w_scr[...], (((1,), (1,)), ((), ())),
                        preferred_element_type=F32) + b_ref[...]
    sc = GATE_SOFTCAP * jnp.tanh(z / GATE_SOFTCAP)
    logf = jnp.minimum(sc, 0.0) - jnp.log1p(jnp.exp(-jnp.abs(sc)))
    rows = z.shape[0]
    row = lax.broadcasted_iota(jnp.int32, z.shape, 0)
    cum = logf
    shift = 1
    while shift < rows:
        cum = cum + jnp.where(row >= shift, pltpu.roll(cum, shift, 0), 0.0)
        shift *= 2
    lane = lax.broadcasted_iota(jnp.int32, z.shape, 1)
    gc = jnp.where(lane < N_HEADS, sc, cum)
    gc_ref[...] = gc
    gr_ref[0] = gc.T[:N_GATE_LANES, :]


def _gates(xn, wt, layer, i_bias, f_bias, chunk):
    n, d = xn.shape
    bias = jnp.pad(jnp.concatenate([i_bias, f_bias]).astype(F32), (0, LANES - N_GATE_LANES))
    return pl.pallas_call(
        functools.partial(_gates_body, layer=layer, row0=_N_MAIN_UNITS * (d // 2)),
        out_shape=(jax.ShapeDtypeStruct((n, LANES), F32),
                   jax.ShapeDtypeStruct((n // chunk, N_GATE_LANES, chunk), F32)),
        grid=(n // chunk,),
        in_specs=[pl.BlockSpec((chunk, d), lambda i: (i, 0)),
                  pl.BlockSpec(memory_space=pl.ANY),
                  pl.BlockSpec((1, LANES), lambda i: (0, 0))],
        out_specs=(pl.BlockSpec((chunk, LANES), lambda i: (i, 0)),
                   pl.BlockSpec((1, N_GATE_LANES, chunk), lambda i: (i, 0, 0))),
        scratch_shapes=[pltpu.VMEM((LANES, d), F32), pltpu.VMEM((LANES, d), BF16),
                        pltpu.SemaphoreType.DMA(())],
        compiler_params=_params(("arbitrary",), _nbytes((chunk, d), BF16),
                                scratch_bytes=_nbytes((LANES, d), F32) + _nbytes((LANES, d), BF16)),
        name="mlstm_gates",
    )(xn, wt, bias.reshape(1, LANES))


def _mlstm_body(q_ref, k_ref, v_ref, o_ref, gc_ref, gr_ref, g_ref, y_ref, c_scr, n_scr, m_scr,
                *, dqk, dv):
    @pl.when(pl.program_id(1) == 0)
    def _():
        c_scr[...] = jnp.zeros_like(c_scr)
        n_scr[...] = jnp.zeros_like(n_scr)
        m_scr[...] = jnp.zeros_like(m_scr)

    chunk = q_ref.shape[0]
    t_idx = lax.broadcasted_iota(jnp.int32, (chunk, chunk), 0)
    s_idx = lax.broadcasted_iota(jnp.int32, (chunk, chunk), 1)
    causal = s_idx <= t_idx
    gc = gc_ref[...]
    gr = gr_ref[0]
    scale = dqk ** -0.5
    for h in range(N_HEADS):
        q = q_ref[:, h * dqk:(h + 1) * dqk]
        kf = k_ref[:, h * dqk:(h + 1) * dqk].astype(F32) * scale
        v = v_ref[:, h * dv:(h + 1) * dv]
        i_col, b_col = gc[:, h:h + 1], gc[:, N_HEADS + h:N_HEADS + h + 1]
        i_row, b_row = gr[h:h + 1, :], gr[N_HEADS + h:N_HEADS + h + 1, :]
        m_prev = m_scr[h]
        d_log = jnp.where(causal, b_col - b_row + i_row, MASKED)
        inter = b_col + m_prev
        m_t = jnp.maximum(inter, jnp.max(d_log, axis=-1, keepdims=True))
        w_inter = jnp.exp(inter - m_t)
        s_qk = lax.dot_general(q, kf.astype(BF16), (((1,), (1,)), ((), ())),
                               preferred_element_type=F32) * jnp.exp(d_log - m_t)
        c_mat = c_scr[h]
        n_vec = n_scr[h]
        num = (w_inter * jnp.dot(q, c_mat.astype(BF16), preferred_element_type=F32)
               + jnp.dot(s_qk.astype(BF16), v, preferred_element_type=F32))
        den = (w_inter * jnp.sum(q.astype(F32) * n_vec, axis=-1, keepdims=True)
               + jnp.sum(s_qk, axis=-1, keepdims=True))
        hid = num / jnp.maximum(jnp.abs(den), jnp.exp(-m_t))
        m_new = m_t[chunk - 1:chunk, :]
        b_last = b_col[chunk - 1:chunk, :]
        decay = jnp.exp(b_last + m_prev - m_new)
        kw = kf * jnp.exp(b_last - b_col + i_col - m_new)
        c_scr[h] = decay * c_mat + lax.dot_general(kw.astype(BF16), v, (((0,), (0,)), ((), ())),
                                                   preferred_element_type=F32)
        n_scr[h] = decay * n_vec + jnp.sum(kw, axis=0, keepdims=True)
        m_scr[h] = m_new
        hid = hid * lax.rsqrt(jnp.mean(hid * hid, axis=-1, keepdims=True) + EPS)
        hid = hid * g_ref[:, h * dv:(h + 1) * dv]
        gate = jax.nn.sigmoid(o_ref[:, h * dv:(h + 1) * dv].astype(F32))
        y_ref[:, h * dv:(h + 1) * dv] = (gate * hid).astype(y_ref.dtype)


def _mlstm(y, gc, gr, mh_norm, batch, seq, d, chunk):
    n = batch * seq
    nc = seq // chunk
    dqk, dv = d // 2 // N_HEADS, d // N_HEADS
    rows = lambda col: (lambda b, c: (b * nc + c, col))
    blocks = (2 * _nbytes((chunk, d // 2), BF16) + 3 * _nbytes((chunk, d), BF16)
              + _nbytes((chunk, LANES), F32))
    state = _nbytes((N_HEADS, dqk, dv), F32)
    return pl.pallas_call(
        functools.partial(_mlstm_body, dqk=dqk, dv=dv),
        out_shape=jax.ShapeDtypeStruct((n, d), BF16),
        grid=(batch, nc),
        in_specs=[pl.BlockSpec((chunk, d // 2), rows(_COL_Q)),
                  pl.BlockSpec((chunk, d // 2), rows(_COL_K)),
                  pl.BlockSpec((chunk, d), rows(_COL_V)),
                  pl.BlockSpec((chunk, d), rows(_COL_O)),
                  pl.BlockSpec((chunk, LANES), rows(0)),
                  pl.BlockSpec((1, N_GATE_LANES, chunk), lambda b, c: (b * nc + c, 0, 0)),
                  pl.BlockSpec((1, d), lambda b, c: (0, 0))],
        out_specs=pl.BlockSpec((chunk, d), rows(0)),
        scratch_shapes=[pltpu.VMEM((N_HEADS, dqk, dv), F32),
                        pltpu.VMEM((N_HEADS, 1, dqk), F32),
                        pltpu.VMEM((N_HEADS, 1, 1), F32)],
        compiler_params=_params(("parallel", "arbitrary"), blocks, scratch_bytes=state,
                                temp_bytes=8 << 20),
        name="mlstm",
    )(y, y, y, y, gc, gr, mh_norm.astype(F32).reshape(1, d))


def _conv_body(bg_ref, cg_ref, vc_ref, cgp_ref, vcp_ref, w_ref, b_ref, o_ref, *, tiles_per_seq):
    u = cg_ref[...].astype(F32) * vc_ref[...].astype(F32)
    keep = (pl.program_id(0) % tiles_per_seq != 0).astype(F32)
    halo = cgp_ref[...].astype(F32) * vcp_ref[...].astype(F32) * keep
    prev1 = halo[BF16_SUBLANES - 1:BF16_SUBLANES, :]
    prev2 = halo[BF16_SUBLANES - 2:BF16_SUBLANES - 1, :]
    row = lax.broadcasted_iota(jnp.int32, u.shape, 0)
    u1 = jnp.where(row == 0, prev1, pltpu.roll(u, 1, 0))
    u2 = jnp.where(row == 0, prev2, jnp.where(row == 1, prev1, pltpu.roll(u, 2, 0)))
    w = w_ref[...]
    conv = w[0:1, :] * u2 + w[1:2, :] * u1 + w[2:3, :] * u + b_ref[...]
    o_ref[...] = (bg_ref[...].astype(F32) * conv).astype(o_ref.dtype)


def _gated_conv(y, conv_w, conv_b, seq, d):
    n = y.shape[0]
    dc = d // 2
    bm, bc = _tile(seq, 512), _tile(dc, 512)
    ncol = dc // bc
    halo_rows = bm // BF16_SUBLANES
    cur = lambda col: (lambda i, j: (i, col * ncol + j))
    prev = lambda col: (lambda i, j: (jnp.maximum(i * halo_rows - 1, 0), col * ncol + j))
    return pl.pallas_call(
        functools.partial(_conv_body, tiles_per_seq=seq // bm),
        out_shape=jax.ShapeDtypeStruct((n, dc), BF16),
        grid=(n // bm, ncol),
        in_specs=[pl.BlockSpec((bm, bc), cur(_COL_BG)),
                  pl.BlockSpec((bm, bc), cur(_COL_CG)),
                  pl.BlockSpec((bm, bc), cur(_COL_VC)),
                  pl.BlockSpec((BF16_SUBLANES, bc), prev(_COL_CG)),
                  pl.BlockSpec((BF16_SUBLANES, bc), prev(_COL_VC)),
                  pl.BlockSpec((conv_w.shape[0], bc), lambda i, j: (0, j)),
                  pl.BlockSpec((1, bc), lambda i, j: (0, j))],
        out_specs=pl.BlockSpec((bm, bc), lambda i, j: (i, j)),
        compiler_params=_params(("parallel", "parallel"), 4 * _nbytes((bm, bc), BF16),
                                temp_bytes=8 * _nbytes((bm, bc), F32)),
        name="gated_conv",
    )(y, y, y, y, y, conv_w.astype(F32), conv_b.astype(F32).reshape(1, dc))


def _merge_body(ya_ref, yb_ref, ga_ref, gb_ref, pc_hbm, pm_hbm, o_ref, pc_stage, pm_stage, pc_scr, pm_scr,
                sems, *, layer, bn, n_blocks):
    def copy_conv(j):
        return pltpu.make_async_copy(pc_hbm.at[layer, :, pl.ds(j * bn, bn)], pc_stage, sems.at[0])

    def copy_mlstm(j):
        return pltpu.make_async_copy(pm_hbm.at[layer, :, pl.ds(j * bn, bn)], pm_stage, sems.at[1])

    _refresh_weights([copy_conv, copy_mlstm], [pc_stage, pm_stage], [pc_scr, pm_scr], n_blocks)
    a = jnp.dot(ya_ref[...], pc_scr[...], preferred_element_type=F32)
    b = jnp.dot(yb_ref[...], pm_scr[...], preferred_element_type=F32)
    merged = (jax.nn.sigmoid(ga_ref[...].astype(F32)) * a
              + jax.nn.sigmoid(gb_ref[...].astype(F32)) * b)
    o_ref[...] = merged.astype(o_ref.dtype)


def _merge(y_a, y_b, p_conv3, p_mlstm3, layer, y, d, bm_pref=1024, bn_pref=512):
    n = y_a.shape[0]
    bm, bn = _tile(n, bm_pref), _tile(d, bn_pref)
    n_blocks = d // bn
    blocks = _nbytes((bm, d // 2), BF16) + _nbytes((bm, d), BF16) + 3 * _nbytes((bm, bn), BF16)
    scratch = (_nbytes((d // 2, bn), F32) + _nbytes((d, bn), F32)
               + _nbytes((d // 2, bn), BF16) + _nbytes((d, bn), BF16))
    return pl.pallas_call(
        functools.partial(_merge_body, layer=layer, bn=bn, n_blocks=n_blocks),
        out_shape=jax.ShapeDtypeStruct((n, d), BF16),
        grid=(n_blocks, n // bm),
        in_specs=[pl.BlockSpec((bm, d // 2), lambda j, i: (i, 0)),
                  pl.BlockSpec((bm, d), lambda j, i: (i, 0)),
                  pl.BlockSpec((bm, bn), lambda j, i: (i, _COL_GA * n_blocks + j)),
                  pl.BlockSpec((bm, bn), lambda j, i: (i, _COL_GB * n_blocks + j)),
                  pl.BlockSpec(memory_space=pl.ANY),
                  pl.BlockSpec(memory_space=pl.ANY)],
        out_specs=pl.BlockSpec((bm, bn), lambda j, i: (i, j)),
        scratch_shapes=[pltpu.VMEM((d // 2, bn), F32), pltpu.VMEM((d, bn), F32),
                        pltpu.VMEM((d // 2, bn), BF16), pltpu.VMEM((d, bn), BF16),
                        pltpu.SemaphoreType.DMA((2,))],
        compiler_params=_params(("arbitrary", "arbitrary"), blocks, scratch_bytes=scratch,
                                temp_bytes=3 * _nbytes((bm, bn), F32)),
        name="merge",
    )(y_a, y_b, y, y, p_conv3, p_mlstm3)


def _hybrid_mixer(x, ln, wt_in, layer, conv_w, conv_b, i_bias, f_bias, mh_norm, p_conv, p_mlstm, w_out,
                  batch, seq):
    n, d = x.shape
    chunk = _tile(seq, MLSTM_CHUNK)
    xn = _rmsnorm(x, ln, BF16)
    y = _in_proj(xn, wt_in, layer, d)
    gc, gr = _gates(xn, wt_in, layer, i_bias, f_bias, chunk)
    y_b = _mlstm(y, gc, gr, mh_norm, batch, seq, d, chunk)
    y_a = _gated_conv(y, conv_w, conv_b, seq, d)
    merged = _merge(y_a, y_b, p_conv, p_mlstm, layer, y, d)
    return _matmul_residual(merged, w_out, layer, x)


_R_E1, _R_E2, _R_RANK1, _R_RANK2, _R_W1, _R_W2 = 0, 1, 2, 3, 4, 5


def _route_body(lg_ref, r_ref, cnt_ref, carry, *, n_exp):
    @pl.when(pl.program_id(0) == 0)
    def _():
        carry[...] = jnp.zeros_like(carry)

    lg = lg_ref[...]
    bt = lg.shape[0]
    lane_i = lax.broadcasted_iota(jnp.int32, lg.shape, 1)
    lane = lane_i.astype(F32)
    x1 = jnp.where(lane_i < n_exp, lg, MASKED)
    m1 = jnp.max(x1, axis=-1, keepdims=True)
    e1 = jnp.min(jnp.where(x1 == m1, lane, float(LANES)), axis=-1, keepdims=True)
    x2 = jnp.where(lane == e1, MASKED, x1)
    m2 = jnp.max(x2, axis=-1, keepdims=True)
    e2 = jnp.min(jnp.where(x2 == m2, lane, float(LANES)), axis=-1, keepdims=True)
    ex = jnp.exp(m2 - m1)
    w1 = 1.0 / (1.0 + ex)
    w2 = ex / (1.0 + ex)
    hit1, hit2 = lane == e1, lane == e2
    onehot = (hit1 | hit2).astype(F32)
    t_idx = lax.broadcasted_iota(jnp.int32, (bt, bt), 0)
    s_idx = lax.broadcasted_iota(jnp.int32, (bt, bt), 1)
    earlier = (s_idx < t_idx).astype(BF16)
    rank = jnp.dot(earlier, onehot.astype(BF16), preferred_element_type=F32) + carry[...]
    rank1 = jnp.sum(jnp.where(hit1, rank, 0.0), axis=-1, keepdims=True)
    rank2 = jnp.sum(jnp.where(hit2, rank, 0.0), axis=-1, keepdims=True)
    carry[...] += jnp.sum(onehot, axis=0, keepdims=True)
    rec = jnp.zeros_like(lg)
    for idx, val in ((_R_E1, e1), (_R_E2, e2), (_R_RANK1, rank1), (_R_RANK2, rank2),
                     (_R_W1, w1), (_R_W2, w2)):
        rec = jnp.where(lane_i == idx, val, rec)
    r_ref[...] = rec
    cnt_ref[...] = jnp.broadcast_to(carry[...], cnt_ref.shape)


def _route(logits, n_exp):
    n = logits.shape[0]
    bt = _tile(n, 512)
    return pl.pallas_call(
        functools.partial(_route_body, n_exp=n_exp),
        out_shape=(jax.ShapeDtypeStruct((n, LANES), F32), jax.ShapeDtypeStruct((8, LANES), F32)),
        grid=(n // bt,),
        in_specs=[pl.BlockSpec((bt, LANES), lambda i: (i, 0))],
        out_specs=(pl.BlockSpec((bt, LANES), lambda i: (i, 0)),
                   pl.BlockSpec((8, LANES), lambda i: (0, 0))),
        scratch_shapes=[pltpu.VMEM((1, LANES), F32)],
        compiler_params=_params(("arbitrary",), 2 * _nbytes((bt, LANES), F32),
                                temp_bytes=_nbytes((bt, bt), F32)),
        name="moe_route",
    )(logits)


def _row_copy(src_hbm, src_row, dst_ref, dst_row, sem):
    return pltpu.make_async_copy(src_hbm.at[pl.ds(src_row, 1)], dst_ref.at[pl.ds(dst_row, 1)], sem)


def _invert_slots_body(pos1_ref, pos2_ref, tok_ref, *, n, n_slots):
    def clear(s, carry):
        tok_ref[s] = 0
        return carry

    def put(t, carry):
        tok_ref[pos1_ref[t]] = t
        tok_ref[pos2_ref[t]] = t
        return carry

    lax.fori_loop(0, n_slots, clear, 0, unroll=8)
    lax.fori_loop(0, n, put, 0, unroll=8)


def _invert_slots(pos1, pos2, n_slots):
    smem = pl.BlockSpec(memory_space=pltpu.SMEM)
    return pl.pallas_call(
        functools.partial(_invert_slots_body, n=pos1.shape[0], n_slots=n_slots),
        out_shape=jax.ShapeDtypeStruct((n_slots,), jnp.int32),
        in_specs=[smem, smem],
        out_specs=smem,
        name="moe_invert",
    )(pos1, pos2)


def _gather_rows_body(tok_ref, h_hbm, o_ref, buf, sems, *, bt, n_steps):
    i = pl.program_id(0)
    slot = i % 2

    def issue(step, to_slot):
        def one(r, carry):
            _row_copy(h_hbm, tok_ref[step * bt + r], buf.at[to_slot], r, sems.at[to_slot]).start()
            return carry
        lax.fori_loop(0, bt, one, 0, unroll=8)

    @pl.when(i == 0)
    def _():
        issue(0, 0)

    @pl.when(i + 1 < n_steps)
    def _():
        issue(i + 1, 1 - slot)

    def drain(r, carry):
        _row_copy(h_hbm, 0, buf.at[slot], 0, sems.at[slot]).wait()
        return carry

    lax.fori_loop(0, bt, drain, 0, unroll=8)
    o_ref[...] = buf[slot].astype(o_ref.dtype)


def _gather_rows(h, tok, n_slots):
    d = h.shape[1]
    bt = _tile(n_slots, 256)
    n_steps = n_slots // bt
    return pl.pallas_call(
        functools.partial(_gather_rows_body, bt=bt, n_steps=n_steps),
        out_shape=jax.ShapeDtypeStruct((n_slots, d), BF16),
        grid_spec=pltpu.PrefetchScalarGridSpec(
            num_scalar_prefetch=1,
            grid=(n_steps,),
            in_specs=[pl.BlockSpec(memory_space=pl.ANY)],
            out_specs=pl.BlockSpec((bt, d), lambda i, tok: (i, 0)),
            scratch_shapes=[pltpu.VMEM((2, bt, d), F32), pltpu.SemaphoreType.DMA((2,))]),
        compiler_params=_params(("arbitrary",), _nbytes((bt, d), BF16),
                                scratch_bytes=_nbytes((2, bt, d), F32), temp_bytes=_nbytes((bt, d), F32)),
        name="moe_gather",
    )(tok, h)


def _refresh_expert_weights(te_ref, nt_ref, nx_ref, copies, stages, w_scrs, n_blocks):
    j, p = pl.program_id(0), pl.program_id(1)
    live = p < nt_ref[0]
    fresh = live & ((p == 0) | (te_ref[p] != te_ref[jnp.maximum(p - 1, 0)]))

    @pl.when(fresh)
    def _():
        @pl.when((j == 0) & (p == 0))
        def _():
            for copy in copies(te_ref[0], 0):
                copy.start()

        for copy, stage, w_scr in zip(copies(te_ref[p], j), stages, w_scrs):
            copy.wait()
            w_scr[...] = stage[...].astype(BF16)

        more = nx_ref[p] >= 0

        @pl.when(more)
        def _():
            for copy in copies(nx_ref[p], j):
                copy.start()

        @pl.when(jnp.logical_not(more) & (j + 1 < n_blocks))
        def _():
            for copy in copies(te_ref[0], j + 1):
                copy.start()

    return live


def _moe_up_body(te_ref, nt_ref, nx_ref, x_ref, w_hbm, o_ref, g_stage, u_stage, wg_scr, wu_scr, sems,
                 *, layer, bn, n_blocks):
    def copies(e, j):
        return (pltpu.make_async_copy(w_hbm.at[layer, e, :, pl.ds(j * bn, bn)], g_stage, sems.at[0]),
                pltpu.make_async_copy(w_hbm.at[layer, e, :, pl.ds((n_blocks + j) * bn, bn)], u_stage,
                                      sems.at[1]))

    live = _refresh_expert_weights(te_ref, nt_ref, nx_ref, copies, [g_stage, u_stage], [wg_scr, wu_scr],
                                   n_blocks)

    @pl.when(live)
    def _():
        o_ref[...] = _swiglu(x_ref[...], wg_scr[...], wu_scr[...]).astype(o_ref.dtype)

    @pl.when(jnp.logical_not(live))
    def _():
        o_ref[...] = jnp.zeros_like(o_ref)


def _moe_up(xs, w4, layer, tiles, bm, bn_pref=512):
    n_slots, d = xs.shape
    f = w4.shape[3] // 2
    bn = _tile(f, bn_pref)
    n_blocks = f // bn
    blocks = _nbytes((bm, d), xs.dtype) + _nbytes((bm, bn), BF16)
    scratch = 2 * (_nbytes((d, bn), F32) + _nbytes((d, bn), BF16))
    return pl.pallas_call(
        functools.partial(_moe_up_body, layer=layer, bn=bn, n_blocks=n_blocks),
        out_shape=jax.ShapeDtypeStruct((n_slots, f), BF16),
        grid_spec=pltpu.PrefetchScalarGridSpec(
            num_scalar_prefetch=3,
            grid=(n_blocks, n_slots // bm),
            in_specs=[pl.BlockSpec((bm, d), lambda j, p, te, nt, nx: (jnp.minimum(p, nt[0] - 1), 0)),
                      pl.BlockSpec(memory_space=pl.ANY)],
            out_specs=pl.BlockSpec((bm, bn), lambda j, p, te, nt, nx: (p, j)),
            scratch_shapes=[pltpu.VMEM((d, bn), F32), pltpu.VMEM((d, bn), F32),
                            pltpu.VMEM((d, bn), BF16), pltpu.VMEM((d, bn), BF16),
                            pltpu.SemaphoreType.DMA((2,))]),
        compiler_params=_params(("arbitrary", "arbitrary"), blocks, scratch_bytes=scratch,
                                temp_bytes=3 * _nbytes((bm, bn), F32)),
        name="moe_up",
    )(*tiles, xs, w4)


def _moe_down_body(te_ref, nt_ref, nx_ref, a_ref, w_hbm, o_ref, stage, w_scr, sem, *, layer, bn, n_blocks):
    def copies(e, j):
        return (pltpu.make_async_copy(w_hbm.at[layer, e, :, pl.ds(j * bn, bn)], stage, sem),)

    live = _refresh_expert_weights(te_ref, nt_ref, nx_ref, copies, [stage], [w_scr], n_blocks)

    @pl.when(live)
    def _():
        o_ref[...] = jnp.dot(a_ref[...], w_scr[...], preferred_element_type=F32)

    @pl.when(jnp.logical_not(live))
    def _():
        o_ref[...] = jnp.zeros_like(o_ref)


def _moe_down(act, w4, layer, tiles, bm, bn_pref=512):
    n_slots, f = act.shape
    d = w4.shape[3]
    bn = _tile(d, bn_pref)
    n_blocks = d // bn
    blocks = _nbytes((bm, f), BF16) + _nbytes((bm, bn), F32)
    scratch = _nbytes((f, bn), F32) + _nbytes((f, bn), BF16)
    return pl.pallas_call(
        functools.partial(_moe_down_body, layer=layer, bn=bn, n_blocks=n_blocks),
        out_shape=jax.ShapeDtypeStruct((n_slots, d), F32),
        grid_spec=pltpu.PrefetchScalarGridSpec(
            num_scalar_prefetch=3,
            grid=(n_blocks, n_slots // bm),
            in_specs=[pl.BlockSpec((bm, f), lambda j, p, te, nt, nx: (jnp.minimum(p, nt[0] - 1), 0)),
                      pl.BlockSpec(memory_space=pl.ANY)],
            out_specs=pl.BlockSpec((bm, bn), lambda j, p, te, nt, nx: (p, j)),
            scratch_shapes=[pltpu.VMEM((f, bn), F32), pltpu.VMEM((f, bn), BF16),
                            pltpu.SemaphoreType.DMA(())]),
        compiler_params=_params(("arbitrary", "arbitrary"), blocks, scratch_bytes=scratch,
                                temp_bytes=_nbytes((bm, bn), F32)),
        name="moe_down",
    )(*tiles, act, w4)


def _combine_body(pos1_ref, pos2_ref, r_ref, x_ref, ys_hbm, *rest, bt, n_steps, final_norm):
    if final_norm:
        g_ref, o_ref, buf, sems = rest
    else:
        o_ref, buf, sems = rest
    i = pl.program_id(0)
    slot = i % 2

    def issue(step, to_slot):
        def one(r, carry):
            t = step * bt + r
            _row_copy(ys_hbm, pos1_ref[t], buf.at[to_slot, 0], r, sems.at[to_slot]).start()
            _row_copy(ys_hbm, pos2_ref[t], buf.at[to_slot, 1], r, sems.at[to_slot]).start()
            return carry
        lax.fori_loop(0, bt, one, 0, unroll=8)

    @pl.when(i == 0)
    def _():
        issue(0, 0)

    @pl.when(i + 1 < n_steps)
    def _():
        issue(i + 1, 1 - slot)

    def drain(r, carry):
        _row_copy(ys_hbm, 0, buf.at[slot, 0], 0, sems.at[slot]).wait()
        _row_copy(ys_hbm, 0, buf.at[slot, 1], 0, sems.at[slot]).wait()
        return carry

    lax.fori_loop(0, bt, drain, 0, unroll=8)
    w1 = r_ref[:, _R_W1:_R_W1 + 1]
    w2 = r_ref[:, _R_W2:_R_W2 + 1]
    out = x_ref[...] + (w1 * buf[slot, 0] + w2 * buf[slot, 1])
    if final_norm:
        out = (out * lax.rsqrt(jnp.mean(out * out, axis=-1, keepdims=True) + EPS)) * g_ref[...]
    o_ref[...] = out


def _combine(x, ys, route, pos1, pos2, final_gain=None):
    n, d = x.shape
    bt = _tile(n, 256)
    n_steps = n // bt
    final_norm = final_gain is not None
    in_specs = [pl.BlockSpec((bt, LANES), lambda i, p1, p2: (i, 0)),
                pl.BlockSpec((bt, d), lambda i, p1, p2: (i, 0)),
                pl.BlockSpec(memory_space=pl.ANY)]
    args = [pos1, pos2, route, x, ys]
    if final_norm:
        in_specs.append(pl.BlockSpec((1, d), lambda i, p1, p2: (0, 0)))
        args.append(final_gain.astype(F32).reshape(1, d))
    return pl.pallas_call(
        functools.partial(_combine_body, bt=bt, n_steps=n_steps, final_norm=final_norm),
        out_shape=jax.ShapeDtypeStruct((n, d), F32),
        grid_spec=pltpu.PrefetchScalarGridSpec(
            num_scalar_prefetch=2,
            grid=(n_steps,),
            in_specs=in_specs,
            out_specs=pl.BlockSpec((bt, d), lambda i, p1, p2: (i, 0)),
            scratch_shapes=[pltpu.VMEM((2, TOP_K, bt, d), F32), pltpu.SemaphoreType.DMA((2,))]),
        compiler_params=_params(("arbitrary",), 2 * _nbytes((bt, d), F32),
                                scratch_bytes=_nbytes((2, TOP_K, bt, d), F32),
                                temp_bytes=2 * _nbytes((bt, d), F32)),
        name="moe_combine",
    )(*args)


def _moe(x, ln, router_w, router_b, w_gate_up, w_down, layer, final_gain, bm_pref=512):
    n, d = x.shape
    n_exp = router_w.shape[1]
    bm = _tile(n * TOP_K, bm_pref)
    n_tiles_max = (n * TOP_K) // bm + n_exp
    h_f32, logits = _rmsnorm_router(x, ln, router_w, router_b)
    route, counts = _route(logits, n_exp)
    counts = counts[0, :n_exp].astype(jnp.int32)
    sizes = (counts + bm - 1) // bm * bm
    ends = jnp.cumsum(sizes)
    starts = ends - sizes
    e1, e2 = route[:, _R_E1].astype(jnp.int32), route[:, _R_E2].astype(jnp.int32)
    pos1 = starts[e1] + route[:, _R_RANK1].astype(jnp.int32)
    pos2 = starts[e2] + route[:, _R_RANK2].astype(jnp.int32)
    n_tiles = (ends[-1] // bm).astype(jnp.int32).reshape(1)
    tile_start = jnp.arange(n_tiles_max, dtype=jnp.int32) * bm
    tile_expert = jnp.sum(tile_start[:, None] >= ends[None, :], axis=1).astype(jnp.int32)
    tile_expert = jnp.minimum(tile_expert, tile_expert[n_tiles[0] - 1])
    tile = jnp.arange(n_tiles_max, dtype=jnp.int32)
    later_other = ((tile[None, :] > tile[:, None]) & (tile[None, :] < n_tiles[0])
                   & (tile_expert[None, :] != tile_expert[:, None]))
    first_other = jnp.min(jnp.where(later_other, tile[None, :], n_tiles_max), axis=1)
    next_expert = jnp.where(first_other < n_tiles_max,
                            tile_expert[jnp.minimum(first_other, n_tiles_max - 1)], -1).astype(jnp.int32)
    tiles = (tile_expert, n_tiles, next_expert)

    n_slots = n_tiles_max * bm
    xs = _gather_rows(h_f32, _invert_slots(pos1, pos2, n_slots), n_slots)
    act = _moe_up(xs, w_gate_up, layer, tiles, bm)
    ys = _moe_down(act, w_down, layer, tiles, bm)
    return _combine(x, ys, route, pos1, pos2, final_gain)


def kernel(x, ln_mix, ln_ffn, w_in, conv_w, conv_b, i_bias, f_bias, mh_norm, p_conv, p_mlstm, w_out,
           ffn_w_gate_up, ffn_w_down, router_w, router_b, exp_w_gate_up, exp_w_down, final_norm):
    batch, seq, d = x.shape
    depth = ln_mix.shape[0]
    xf = x.reshape(batch * seq, d)
    wt_in = jnp.transpose(w_in, (0, 2, 1))
    normed = False
    for layer in range(depth):
        xf = _hybrid_mixer(xf, ln_mix[layer], wt_in, layer, conv_w[layer], conv_b[layer],
                           i_bias[layer], f_bias[layer], mh_norm[layer], p_conv, p_mlstm, w_out,
                           batch, seq)
        j = layer // 2
        if layer % 2 == 0:
            h = _rmsnorm(xf, ln_ffn[layer], BF16)
            act = _swiglu_up(h, ffn_w_gate_up, j)
            xf = _matmul_residual_ktiled(act, ffn_w_down[j].astype(BF16), xf)
        else:
            normed = layer == depth - 1
            xf = _moe(xf, ln_ffn[layer], router_w[j], router_b[j], exp_w_gate_up, exp_w_down, j,
                      final_norm if normed else None)
    if not normed:
        xf = _rmsnorm(xf, final_norm, x.dtype)
    return xf.reshape(batch, seq, d)
```

```python
import functools

import jax
import jax.numpy as jnp
from jax import lax
from jax.experimental import pallas as pl
from jax.experimental.pallas import tpu as pltpu

F32 = jnp.float32
BF16 = jnp.bfloat16

EPS = 1e-6
GATE_SOFTCAP = 15.0
N_HEADS = 8
TOP_K = 2

V7X_VMEM_BYTES = 64 * 2**20
LANES = 128
BF16_SUBLANES = 16
MASKED = -1e30

MLSTM_CHUNK = 256
HEAD_GROUP = 2
N_GATE_LANES = 2 * N_HEADS


def _tile(dim, pref):
    if dim <= pref:
        return dim
    t = pref
    while dim % t:
        t -= LANES
    return t


def _params(semantics, block_bytes, scratch_bytes=0, temp_bytes=0):
    need = 2 * block_bytes + scratch_bytes + temp_bytes + (4 << 20)
    return pltpu.CompilerParams(
        dimension_semantics=semantics,
        vmem_limit_bytes=int(min(max(need, 32 << 20), V7X_VMEM_BYTES - (6 << 20))))


def _nbytes(shape, dtype):
    n = 1
    for s in shape:
        n *= s
    return n * jnp.dtype(dtype).itemsize


def _rmsnorm_body(x_ref, g_ref, o_ref):
    x = x_ref[...]
    inv = lax.rsqrt(jnp.mean(x * x, axis=-1, keepdims=True) + EPS)
    o_ref[...] = ((x * inv) * g_ref[...]).astype(o_ref.dtype)


def _rmsnorm(x, g, out_dtype):
    n, d = x.shape
    bm = _tile(n, 256)
    return pl.pallas_call(
        _rmsnorm_body,
        out_shape=jax.ShapeDtypeStruct((n, d), out_dtype),
        grid=(n // bm,),
        in_specs=[pl.BlockSpec((bm, d), lambda i: (i, 0)),
                  pl.BlockSpec((1, d), lambda i: (0, 0))],
        out_specs=pl.BlockSpec((bm, d), lambda i: (i, 0)),
        compiler_params=_params(("parallel",), _nbytes((bm, d), F32) * 2),
        name="rmsnorm",
    )(x, g.reshape(1, d))


def _rmsnorm_router_body(x_ref, g_ref, whi_ref, wlo_ref, rb_ref, hf_ref, lg_ref):
    x = x_ref[...]
    inv = lax.rsqrt(jnp.mean(x * x, axis=-1, keepdims=True) + EPS)
    h = (x * inv) * g_ref[...]
    h_hi = h.astype(BF16)
    hf_ref[...] = h
    h_lo = (h - h_hi.astype(F32)).astype(BF16)
    lg = (jnp.dot(h_hi, whi_ref[...], preferred_element_type=F32)
          + jnp.dot(h_lo, whi_ref[...], preferred_element_type=F32)
          + jnp.dot(h_hi, wlo_ref[...], preferred_element_type=F32))
    lg_ref[...] = lg + rb_ref[...]


def _rmsnorm_router(x, g, router_w, router_b):
    n, d = x.shape
    n_exp = router_w.shape[1]
    bm = _tile(n, 256)
    w = jnp.pad(router_w.astype(F32), ((0, 0), (0, LANES - n_exp)))
    w_hi = w.astype(BF16)
    w_lo = (w - w_hi.astype(F32)).astype(BF16)
    rb = jnp.pad(router_b.astype(F32), (0, LANES - n_exp)).reshape(1, LANES)
    row = lambda i: (i, 0)
    fixed = lambda i: (0, 0)
    return pl.pallas_call(
        _rmsnorm_router_body,
        out_shape=(jax.ShapeDtypeStruct((n, d), F32), jax.ShapeDtypeStruct((n, LANES), F32)),
        grid=(n // bm,),
        in_specs=[pl.BlockSpec((bm, d), row), pl.BlockSpec((1, d), fixed),
                  pl.BlockSpec((d, LANES), fixed), pl.BlockSpec((d, LANES), fixed),
                  pl.BlockSpec((1, LANES), fixed)],
        out_specs=(pl.BlockSpec((bm, d), row), pl.BlockSpec((bm, LANES), row)),
        compiler_params=_params(("parallel",), _nbytes((bm, d), F32) * 2 + _nbytes((d, LANES), BF16) * 2),
        name="rmsnorm_router",
    )(x, g.reshape(1, d), w_hi, w_lo, rb)


def _refresh_weights(copies, stages, w_scrs, n_blocks):
    j = pl.program_id(0)

    @pl.when(pl.program_id(1) == 0)
    def _():
        @pl.when(j == 0)
        def _():
            for copy in copies:
                copy(0).start()

        for copy, stage, w_scr in zip(copies, stages, w_scrs):
            copy(j).wait()
            w_scr[...] = stage[...].astype(BF16)

        @pl.when(j + 1 < n_blocks)
        def _():
            for copy in copies:
                copy(j + 1).start()


def _in_proj_body(x_ref, wt_hbm, o_ref, stage, w_scr, sem, *, layer, bn, n_blocks, n_main_blocks):
    def copy(j):
        row0 = j * bn + jnp.where(j >= n_main_blocks, N_GATE_LANES, 0)
        return pltpu.make_async_copy(wt_hbm.at[layer, pl.ds(row0, bn)], stage, sem)

    _refresh_weights([copy], [stage], [w_scr], n_blocks)
    o_ref[...] = lax.dot_general(x_ref[...], w_scr[...], (((1,), (1,)), ((), ())),
                                 preferred_element_type=F32).astype(o_ref.dtype)


def _in_proj(x, wt, layer, d, bm_pref=1024, bn_pref=1024):
    m, k = x.shape
    dc = d // 2
    bm, bn = _tile(m, bm_pref), _tile(dc, bn_pref)
    nb = dc // bn
    n_units = _N_MAIN_UNITS + 4
    n_blocks = n_units * nb
    out_block = lambda j: jnp.where(j < 3 * nb, j + 10 * nb, jnp.where(j < 5 * nb, j + 5 * nb, j - 5 * nb))
    blocks = _nbytes((bm, k), x.dtype) + _nbytes((bm, bn), BF16)
    scratch = _nbytes((bn, k), F32) + _nbytes((bn, k), BF16)
    return pl.pallas_call(
        functools.partial(_in_proj_body, layer=layer, bn=bn, n_blocks=n_blocks,
                          n_main_blocks=_N_MAIN_UNITS * nb),
        out_shape=jax.ShapeDtypeStruct((m, n_units * dc), BF16),
        grid=(n_blocks, m // bm),
        in_specs=[pl.BlockSpec((bm, k), lambda j, i: (i, 0)),
                  pl.BlockSpec(memory_space=pl.ANY)],
        out_specs=pl.BlockSpec((bm, bn), lambda j, i: (i, out_block(j))),
        scratch_shapes=[pltpu.VMEM((bn, k), F32), pltpu.VMEM((bn, k), BF16), pltpu.SemaphoreType.DMA(())],
        compiler_params=_params(("arbitrary", "arbitrary"), blocks, scratch_bytes=scratch,
                                temp_bytes=_nbytes((bm, bn), F32)),
        name="in_proj",
    )(x, wt)


def _matmul_residual_body(a_ref, w_hbm, x_ref, o_ref, stage, w_scr, sem, *, layer, bn, n_blocks):
    def copy(j):
        return pltpu.make_async_copy(w_hbm.at[layer, :, pl.ds(j * bn, bn)], stage, sem)

    _refresh_weights([copy], [stage], [w_scr], n_blocks)
    o_ref[...] = x_ref[...] + jnp.dot(a_ref[...], w_scr[...], preferred_element_type=F32)


def _matmul_residual(a, w3, layer, x, bm_pref=1024, bn_pref=512):
    m, k = a.shape
    n = w3.shape[2]
    bm, bn = _tile(m, bm_pref), _tile(n, bn_pref)
    n_blocks = n // bn
    blocks = _nbytes((bm, k), a.dtype) + 2 * _nbytes((bm, bn), F32)
    scratch = _nbytes((k, bn), F32) + _nbytes((k, bn), BF16)
    return pl.pallas_call(
        functools.partial(_matmul_residual_body, layer=layer, bn=bn, n_blocks=n_blocks),
        out_shape=jax.ShapeDtypeStruct((m, n), F32),
        grid=(n_blocks, m // bm),
        in_specs=[pl.BlockSpec((bm, k), lambda j, i: (i, 0)),
                  pl.BlockSpec(memory_space=pl.ANY),
                  pl.BlockSpec((bm, bn), lambda j, i: (i, j))],
        out_specs=pl.BlockSpec((bm, bn), lambda j, i: (i, j)),
        scratch_shapes=[pltpu.VMEM((k, bn), F32), pltpu.VMEM((k, bn), BF16), pltpu.SemaphoreType.DMA(())],
        compiler_params=_params(("arbitrary", "arbitrary"), blocks, scratch_bytes=scratch,
                                temp_bytes=_nbytes((bm, bn), F32)),
        name="out_proj",
    )(a, w3, x)


def _matmul_residual_ktiled_body(a_ref, w_ref, x_ref, o_ref):
    @pl.when(pl.program_id(2) == 0)
    def _():
        o_ref[...] = x_ref[...]

    o_ref[...] += jnp.dot(a_ref[...], w_ref[...], preferred_element_type=F32)


def _matmul_residual_ktiled(a, w, x, bm_pref=1024, bn_pref=1024, bk_pref=3584):
    m, k = a.shape
    n = w.shape[1]
    bm, bn, bk = _tile(m, bm_pref), _tile(n, bn_pref), _tile(k, bk_pref)
    blocks = _nbytes((bm, bk), a.dtype) + _nbytes((bk, bn), w.dtype) + 2 * _nbytes((bm, bn), F32)
    return pl.pallas_call(
        _matmul_residual_ktiled_body,
        out_shape=jax.ShapeDtypeStruct((m, n), F32),
        grid=(m // bm, n // bn, k // bk),
        in_specs=[pl.BlockSpec((bm, bk), lambda i, j, kk: (i, kk)),
                  pl.BlockSpec((bk, bn), lambda i, j, kk: (kk, j)),
                  pl.BlockSpec((bm, bn), lambda i, j, kk: (i, j))],
        out_specs=pl.BlockSpec((bm, bn), lambda i, j, kk: (i, j)),
        compiler_params=_params(("parallel", "parallel", "arbitrary"), blocks,
                                temp_bytes=_nbytes((bm, bn), F32)),
        name="ffn_down",
    )(a, w, x)


def _swiglu(h, wg, wu):
    g = jnp.dot(h, wg, preferred_element_type=F32)
    u = jnp.dot(h, wu, preferred_element_type=F32)
    return g * jax.nn.sigmoid(g) * u


def _swiglu_up_body(h_ref, w_hbm, o_ref, g_stage, u_stage, wg_scr, wu_scr, sems, *, layer, bn, n_blocks):
    def copy_gate(j):
        return pltpu.make_async_copy(w_hbm.at[layer, :, pl.ds(j * bn, bn)], g_stage, sems.at[0])

    def copy_up(j):
        return pltpu.make_async_copy(w_hbm.at[layer, :, pl.ds((n_blocks + j) * bn, bn)], u_stage, sems.at[1])

    _refresh_weights([copy_gate, copy_up], [g_stage, u_stage], [wg_scr, wu_scr], n_blocks)
    o_ref[...] = _swiglu(h_ref[...], wg_scr[...], wu_scr[...]).astype(o_ref.dtype)


def _swiglu_up(h, w3, layer, bm_pref=1024, bn_pref=512):
    m, k = h.shape
    f = w3.shape[2] // 2
    bm, bn = _tile(m, bm_pref), _tile(f, bn_pref)
    n_blocks = f // bn
    blocks = _nbytes((bm, k), h.dtype) + _nbytes((bm, bn), BF16)
    scratch = 2 * (_nbytes((k, bn), F32) + _nbytes((k, bn), BF16))
    return pl.pallas_call(
        functools.partial(_swiglu_up_body, layer=layer, bn=bn, n_blocks=n_blocks),
        out_shape=jax.ShapeDtypeStruct((m, f), BF16),
        grid=(n_blocks, m // bm),
        in_specs=[pl.BlockSpec((bm, k), lambda j, i: (i, 0)),
                  pl.BlockSpec(memory_space=pl.ANY)],
        out_specs=pl.BlockSpec((bm, bn), lambda j, i: (i, j)),
        scratch_shapes=[pltpu.VMEM((k, bn), F32), pltpu.VMEM((k, bn), F32),
                        pltpu.VMEM((k, bn), BF16), pltpu.VMEM((k, bn), BF16),
                        pltpu.SemaphoreType.DMA((2,))],
        compiler_params=_params(("arbitrary", "arbitrary"), blocks, scratch_bytes=scratch,
                                temp_bytes=3 * _nbytes((bm, bn), F32)),
        name="ffn_up",
    )(h, w3)


_COL_V, _COL_O, _COL_GA, _COL_GB = 0, 1, 2, 3
_COL_Q, _COL_K, _COL_BG, _COL_CG, _COL_VC = 8, 9, 10, 11, 12
_N_MAIN_UNITS = 9


def _gates_body(x_ref, wt_hbm, b_ref, gc_ref, gr_ref, stage, w_scr, sem, *, layer, row0):
    @pl.when(pl.program_id(0) == 0)
    def _():
        stage[...] = jnp.zeros_like(stage)
        rows = pltpu.make_async_copy(wt_hbm.at[layer, pl.ds(row0, N_GATE_LANES)],
                                     stage.at[pl.ds(0, N_GATE_LANES)], sem)
        rows.start()
        rows.wait()
        w_scr[...] = stage[...].astype(BF16)

    z = lax.dot_general(x_ref[...], w_scr[...], (((1,), (1,)), ((), ())),
                        preferred_element_type=F32) + b_ref[...]
    sc = GATE_SOFTCAP * jnp.tanh(z / GATE_SOFTCAP)
    logf = jnp.minimum(sc, 0.0) - jnp.log1p(jnp.exp(-jnp.abs(sc)))
    rows = z.shape[0]
    row = lax.broadcasted_iota(jnp.int32, z.shape, 0)
    cum = logf
    shift = 1
    while shift < rows:
        cum = cum + jnp.where(row >= shift, pltpu.roll(cum, shift, 0), 0.0)
        shift *= 2
    lane = lax.broadcasted_iota(jnp.int32, z.shape, 1)
    gc = jnp.where(lane < N_HEADS, sc, cum)
    gc_ref[...] = gc
    gr_ref[0] = gc.T[:N_GATE_LANES, :]


def _gates(xn, wt, layer, i_bias, f_bias, chunk):
    n, d = xn.shape
    bias = jnp.pad(jnp.concatenate([i_bias, f_bias]).astype(F32), (0, LANES - N_GATE_LANES))
    return pl.pallas_call(
        functools.partial(_gates_body, layer=layer, row0=_N_MAIN_UNITS * (d // 2)),
        out_shape=(jax.ShapeDtypeStruct((n, LANES), F32),
                   jax.ShapeDtypeStruct((n // chunk, N_GATE_LANES, chunk), F32)),
        grid=(n // chunk,),
        in_specs=[pl.BlockSpec((chunk, d), lambda i: (i, 0)),
                  pl.BlockSpec(memory_space=pl.ANY),
                  pl.BlockSpec((1, LANES), lambda i: (0, 0))],
        out_specs=(pl.BlockSpec((chunk, LANES), lambda i: (i, 0)),
                   pl.BlockSpec((1, N_GATE_LANES, chunk), lambda i: (i, 0, 0))),
        scratch_shapes=[pltpu.VMEM((LANES, d), F32), pltpu.VMEM((LANES, d), BF16),
                        pltpu.SemaphoreType.DMA(())],
        compiler_params=_params(("arbitrary",), _nbytes((chunk, d), BF16),
                                scratch_bytes=_nbytes((LANES, d), F32) + _nbytes((LANES, d), BF16)),
        name="mlstm_gates",
    )(xn, wt, bias.reshape(1, LANES))


def _wide(col, n):
    return col if n == LANES else jnp.concatenate([col] * (n // LANES), axis=1)


def _mlstm_body(q_ref, k_ref, v_ref, o_ref, gc_ref, gr_ref, g_ref, y_ref, c_scr, n_scr, m_scr,
                *, dqk, dv):
    @pl.when(pl.program_id(1) == 0)
    def _():
        c_scr[...] = jnp.zeros_like(c_scr)
        n_scr[...] = jnp.zeros_like(n_scr)
        m_scr[...] = jnp.zeros_like(m_scr)

    chunk = q_ref.shape[0]
    t_idx = lax.broadcasted_iota(jnp.int32, (chunk, chunk), 0)
    s_idx = lax.broadcasted_iota(jnp.int32, (chunk, chunk), 1)
    causal = s_idx <= t_idx
    gc = gc_ref[...]
    gr = gr_ref[0]
    scale = dqk ** -0.5
    for h0 in range(0, N_HEADS, HEAD_GROUP):
        heads = range(h0, h0 + HEAD_GROUP)
        st = {h: {} for h in heads}
        for h in heads:
            t = st[h]
            t["q"] = q_ref[:, h * dqk:(h + 1) * dqk]
            t["kf"] = k_ref[:, h * dqk:(h + 1) * dqk].astype(F32) * scale
            t["v"] = v_ref[:, h * dv:(h + 1) * dv]
            t["i_col"] = jnp.broadcast_to(gc[:, h:h + 1], (chunk, LANES))
            t["b_col"] = jnp.broadcast_to(gc[:, N_HEADS + h:N_HEADS + h + 1], (chunk, LANES))
            i_row, b_row = gr[h:h + 1, :], gr[N_HEADS + h:N_HEADS + h + 1, :]
            t["m_prev"] = m_scr[h]
            t["d_log"] = jnp.where(causal, _wide(t["b_col"], chunk) - b_row + i_row, MASKED)
            t["inter"] = t["b_col"] + t["m_prev"]
        for h in heads:
            t = st[h]
            t["m_t"] = jnp.maximum(t["inter"], jnp.max(t["d_log"], axis=-1, keepdims=True))
            t["w_inter"] = jnp.exp(t["inter"] - t["m_t"])
        for h in heads:
            t = st[h]
            t["s_qk"] = (lax.dot_general(t["q"], t["kf"].astype(BF16), (((1,), (1,)), ((), ())),
                                         preferred_element_type=F32)
                         * jnp.exp(t["d_log"] - _wide(t["m_t"], chunk)))
        for h in heads:
            t = st[h]
            t["c_mat"] = c_scr[h]
            t["n_vec"] = n_scr[h]
            t["num"] = (_wide(t["w_inter"], dv)
                        * jnp.dot(t["q"], t["c_mat"].astype(BF16), preferred_element_type=F32)
                        + jnp.dot(t["s_qk"].astype(BF16), t["v"], preferred_element_type=F32))
            t["den"] = (t["w_inter"] * jnp.sum(t["q"].astype(F32) * t["n_vec"], axis=-1, keepdims=True)
                        + jnp.sum(t["s_qk"], axis=-1, keepdims=True))
        for h in heads:
            t = st[h]
            m_new = t["m_t"][chunk - 1:chunk, :1]
            b_last = t["b_col"][chunk - 1:chunk, :1]
            decay = jnp.exp(b_last + t["m_prev"] - m_new)
            kw = t["kf"] * _wide(jnp.exp(b_last - t["b_col"] + t["i_col"] - m_new), dqk)
            c_scr[h] = decay * t["c_mat"] + lax.dot_general(kw.astype(BF16), t["v"], (((0,), (0,)), ((), ())),
                                                            preferred_element_type=F32)
            n_scr[h] = decay * t["n_vec"] + jnp.sum(kw, axis=0, keepdims=True)
            m_scr[h] = m_new
        for h in heads:
            t = st[h]
            hid = t["num"] * _wide(1.0 / jnp.maximum(jnp.abs(t["den"]), jnp.exp(-t["m_t"])), dv)
            hid = hid * lax.rsqrt(jnp.mean(hid * hid, axis=-1, keepdims=True) + EPS)
            hid = hid * g_ref[:, h * dv:(h + 1) * dv]
            gate = jax.nn.sigmoid(o_ref[:, h * dv:(h + 1) * dv].astype(F32))
            y_ref[:, h * dv:(h + 1) * dv] = (gate * hid).astype(y_ref.dtype)


def _mlstm(y, gc, gr, mh_norm, batch, seq, d, chunk):
    n = batch * seq
    nc = seq // chunk
    dqk, dv = d // 2 // N_HEADS, d // N_HEADS
    rows = lambda col: (lambda b, c: (b * nc + c, col))
    blocks = (2 * _nbytes((chunk, d // 2), BF16) + 3 * _nbytes((chunk, d), BF16)
              + _nbytes((chunk, LANES), F32))
    state = _nbytes((N_HEADS, dqk, dv), F32)
    return pl.pallas_call(
        functools.partial(_mlstm_body, dqk=dqk, dv=dv),
        out_shape=jax.ShapeDtypeStruct((n, d), BF16),
        grid=(batch, nc),
        in_specs=[pl.BlockSpec((chunk, d // 2), rows(_COL_Q)),
                  pl.BlockSpec((chunk, d // 2), rows(_COL_K)),
                  pl.BlockSpec((chunk, d), rows(_COL_V)),
                  pl.BlockSpec((chunk, d), rows(_COL_O)),
                  pl.BlockSpec((chunk, LANES), rows(0)),
                  pl.BlockSpec((1, N_GATE_LANES, chunk), lambda b, c: (b * nc + c, 0, 0)),
                  pl.BlockSpec((1, d), lambda b, c: (0, 0))],
        out_specs=pl.BlockSpec((chunk, d), rows(0)),
        scratch_shapes=[pltpu.VMEM((N_HEADS, dqk, dv), F32),
                        pltpu.VMEM((N_HEADS, 1, dqk), F32),
                        pltpu.VMEM((N_HEADS, 1, 1), F32)],
        compiler_params=_params(("parallel", "arbitrary"), blocks, scratch_bytes=state,
                                temp_bytes=8 << 20),
        name="mlstm",
    )(y, y, y, y, gc, gr, mh_norm.astype(F32).reshape(1, d))


def _conv_body(bg_ref, cg_ref, vc_ref, cgp_ref, vcp_ref, w_ref, b_ref, o_ref, *, tiles_per_seq):
    u = cg_ref[...].astype(F32) * vc_ref[...].astype(F32)
    keep = (pl.program_id(0) % tiles_per_seq != 0).astype(F32)
    halo = cgp_ref[...].astype(F32) * vcp_ref[...].astype(F32) * keep
    prev1 = halo[BF16_SUBLANES - 1:BF16_SUBLANES, :]
    prev2 = halo[BF16_SUBLANES - 2:BF16_SUBLANES - 1, :]
    row = lax.broadcasted_iota(jnp.int32, u.shape, 0)
    u1 = jnp.where(row == 0, prev1, pltpu.roll(u, 1, 0))
    u2 = jnp.where(row == 0, prev2, jnp.where(row == 1, prev1, pltpu.roll(u, 2, 0)))
    w = w_ref[...]
    conv = w[0:1, :] * u2 + w[1:2, :] * u1 + w[2:3, :] * u + b_ref[...]
    o_ref[...] = (bg_ref[...].astype(F32) * conv).astype(o_ref.dtype)


def _gated_conv(y, conv_w, conv_b, seq, d):
    n = y.shape[0]
    dc = d // 2
    bm, bc = _tile(seq, 512), _tile(dc, 512)
    ncol = dc // bc
    halo_rows = bm // BF16_SUBLANES
    cur = lambda col: (lambda i, j: (i, col * ncol + j))
    prev = lambda col: (lambda i, j: (jnp.maximum(i * halo_rows - 1, 0), col * ncol + j))
    return pl.pallas_call(
        functools.partial(_conv_body, tiles_per_seq=seq // bm),
        out_shape=jax.ShapeDtypeStruct((n, dc), BF16),
        grid=(n // bm, ncol),
        in_specs=[pl.BlockSpec((bm, bc), cur(_COL_BG)),
                  pl.BlockSpec((bm, bc), cur(_COL_CG)),
                  pl.BlockSpec((bm, bc), cur(_COL_VC)),
                  pl.BlockSpec((BF16_SUBLANES, bc), prev(_COL_CG)),
                  pl.BlockSpec((BF16_SUBLANES, bc), prev(_COL_VC)),
                  pl.BlockSpec((conv_w.shape[0], bc), lambda i, j: (0, j)),
                  pl.BlockSpec((1, bc), lambda i, j: (0, j))],
        out_specs=pl.BlockSpec((bm, bc), lambda i, j: (i, j)),
        compiler_params=_params(("parallel", "parallel"), 4 * _nbytes((bm, bc), BF16),
                                temp_bytes=8 * _nbytes((bm, bc), F32)),
        name="gated_conv",
    )(y, y, y, y, y, conv_w.astype(F32), conv_b.astype(F32).reshape(1, dc))


def _merge_body(ya_ref, yb_ref, ga_ref, gb_ref, pc_hbm, pm_hbm, o_ref, pc_stage, pm_stage, pc_scr, pm_scr,
                sems, *, layer, bn, n_blocks):
    def copy_conv(j):
        return pltpu.make_async_copy(pc_hbm.at[layer, :, pl.ds(j * bn, bn)], pc_stage, sems.at[0])

    def copy_mlstm(j):
        return pltpu.make_async_copy(pm_hbm.at[layer, :, pl.ds(j * bn, bn)], pm_stage, sems.at[1])

    _refresh_weights([copy_conv, copy_mlstm], [pc_stage, pm_stage], [pc_scr, pm_scr], n_blocks)
    a = jnp.dot(ya_ref[...], pc_scr[...], preferred_element_type=F32)
    b = jnp.dot(yb_ref[...], pm_scr[...], preferred_element_type=F32)
    merged = (jax.nn.sigmoid(ga_ref[...].astype(F32)) * a
              + jax.nn.sigmoid(gb_ref[...].astype(F32)) * b)
    o_ref[...] = merged.astype(o_ref.dtype)


def _merge(y_a, y_b, p_conv3, p_mlstm3, layer, y, d, bm_pref=1024, bn_pref=512):
    n = y_a.shape[0]
    bm, bn = _tile(n, bm_pref), _tile(d, bn_pref)
    n_blocks = d // bn
    blocks = _nbytes((bm, d // 2), BF16) + _nbytes((bm, d), BF16) + 3 * _nbytes((bm, bn), BF16)
    scratch = (_nbytes((d // 2, bn), F32) + _nbytes((d, bn), F32)
               + _nbytes((d // 2, bn), BF16) + _nbytes((d, bn), BF16))
    return pl.pallas_call(
        functools.partial(_merge_body, layer=layer, bn=bn, n_blocks=n_blocks),
        out_shape=jax.ShapeDtypeStruct((n, d), BF16),
        grid=(n_blocks, n // bm),
        in_specs=[pl.BlockSpec((bm, d // 2), lambda j, i: (i, 0)),
                  pl.BlockSpec((bm, d), lambda j, i: (i, 0)),
                  pl.BlockSpec((bm, bn), lambda j, i: (i, _COL_GA * n_blocks + j)),
                  pl.BlockSpec((bm, bn), lambda j, i: (i, _COL_GB * n_blocks + j)),
                  pl.BlockSpec(memory_space=pl.ANY),
                  pl.BlockSpec(memory_space=pl.ANY)],
        out_specs=pl.BlockSpec((bm, bn), lambda j, i: (i, j)),
        scratch_shapes=[pltpu.VMEM((d // 2, bn), F32), pltpu.VMEM((d, bn), F32),
                        pltpu.VMEM((d // 2, bn), BF16), pltpu.VMEM((d, bn), BF16),
                        pltpu.SemaphoreType.DMA((2,))],
        compiler_params=_params(("arbitrary", "arbitrary"), blocks, scratch_bytes=scratch,
                                temp_bytes=3 * _nbytes((bm, bn), F32)),
        name="merge",
    )(y_a, y_b, y, y, p_conv3, p_mlstm3)


def _hybrid_mixer(x, ln, wt_in, layer, conv_w, conv_b, i_bias, f_bias, mh_norm, p_conv, p_mlstm, w_out,
                  batch, seq):
    n, d = x.shape
    chunk = _tile(seq, MLSTM_CHUNK)
    xn = _rmsnorm(x, ln, BF16)
    y = _in_proj(xn, wt_in, layer, d)
    gc, gr = _gates(xn, wt_in, layer, i_bias, f_bias, chunk)
    y_b = _mlstm(y, gc, gr, mh_norm, batch, seq, d, chunk)
    y_a = _gated_conv(y, conv_w, conv_b, seq, d)
    merged = _merge(y_a, y_b, p_conv, p_mlstm, layer, y, d)
    return _matmul_residual(merged, w_out, layer, x)


_R_E1, _R_E2, _R_RANK1, _R_RANK2, _R_W1, _R_W2 = 0, 1, 2, 3, 4, 5


def _route_body(lg_ref, r_ref, cnt_ref, carry, *, n_exp):
    @pl.when(pl.program_id(0) == 0)
    def _():
        carry[...] = jnp.zeros_like(carry)

    lg = lg_ref[...]
    bt = lg.shape[0]
    lane_i = lax.broadcasted_iota(jnp.int32, lg.shape, 1)
    lane = lane_i.astype(F32)
    x1 = jnp.where(lane_i < n_exp, lg, MASKED)
    m1 = jnp.max(x1, axis=-1, keepdims=True)
    e1 = jnp.min(jnp.where(x1 == m1, lane, float(LANES)), axis=-1, keepdims=True)
    x2 = jnp.where(lane == e1, MASKED, x1)
    m2 = jnp.max(x2, axis=-1, keepdims=True)
    e2 = jnp.min(jnp.where(x2 == m2, lane, float(LANES)), axis=-1, keepdims=True)
    ex = jnp.exp(m2 - m1)
    w1 = 1.0 / (1.0 + ex)
    w2 = ex / (1.0 + ex)
    hit1, hit2 = lane == e1, lane == e2
    onehot = (hit1 | hit2).astype(F32)
    t_idx = lax.broadcasted_iota(jnp.int32, (bt, bt), 0)
    s_idx = lax.broadcasted_iota(jnp.int32, (bt, bt), 1)
    earlier = (s_idx < t_idx).astype(BF16)
    rank = jnp.dot(earlier, onehot.astype(BF16), preferred_element_type=F32) + carry[...]
    rank1 = jnp.sum(jnp.where(hit1, rank, 0.0), axis=-1, keepdims=True)
    rank2 = jnp.sum(jnp.where(hit2, rank, 0.0), axis=-1, keepdims=True)
    carry[...] += jnp.sum(onehot, axis=0, keepdims=True)
    rec = jnp.zeros_like(lg)
    for idx, val in ((_R_E1, e1), (_R_E2, e2), (_R_RANK1, rank1), (_R_RANK2, rank2),
                     (_R_W1, w1), (_R_W2, w2)):
        rec = jnp.where(lane_i == idx, val, rec)
    r_ref[...] = rec
    cnt_ref[...] = jnp.broadcast_to(carry[...], cnt_ref.shape)


def _route(logits, n_exp):
    n = logits.shape[0]
    bt = _tile(n, 512)
    return pl.pallas_call(
        functools.partial(_route_body, n_exp=n_exp),
        out_shape=(jax.ShapeDtypeStruct((n, LANES), F32), jax.ShapeDtypeStruct((8, LANES), F32)),
        grid=(n // bt,),
        in_specs=[pl.BlockSpec((bt, LANES), lambda i: (i, 0))],
        out_specs=(pl.BlockSpec((bt, LANES), lambda i: (i, 0)),
                   pl.BlockSpec((8, LANES), lambda i: (0, 0))),
        scratch_shapes=[pltpu.VMEM((1, LANES), F32)],
        compiler_params=_params(("arbitrary",), 2 * _nbytes((bt, LANES), F32),
                                temp_bytes=_nbytes((bt, bt), F32)),
        name="moe_route",
    )(logits)


def _row_copy(src_hbm, src_row, dst_ref, dst_row, sem):
    return pltpu.make_async_copy(src_hbm.at[pl.ds(src_row, 1)], dst_ref.at[pl.ds(dst_row, 1)], sem)


def _invert_slots_body(pos1_ref, pos2_ref, tok_ref, *, n, n_slots):
    def clear(s, carry):
        tok_ref[s] = 0
        return carry

    def put(t, carry):
        tok_ref[pos1_ref[t]] = t
        tok_ref[pos2_ref[t]] = t
        return carry

    lax.fori_loop(0, n_slots, clear, 0, unroll=8)
    lax.fori_loop(0, n, put, 0, unroll=8)


def _invert_slots(pos1, pos2, n_slots):
    smem = pl.BlockSpec(memory_space=pltpu.SMEM)
    return pl.pallas_call(
        functools.partial(_invert_slots_body, n=pos1.shape[0], n_slots=n_slots),
        out_shape=jax.ShapeDtypeStruct((n_slots,), jnp.int32),
        in_specs=[smem, smem],
        out_specs=smem,
        name="moe_invert",
    )(pos1, pos2)


def _gather_rows_body(tok_ref, h_hbm, o_ref, buf, sems, *, bt, n_steps):
    i = pl.program_id(0)
    slot = i % 2

    def issue(step, to_slot):
        def one(r, carry):
            _row_copy(h_hbm, tok_ref[step * bt + r], buf.at[to_slot], r, sems.at[to_slot]).start()
            return carry
        lax.fori_loop(0, bt, one, 0, unroll=8)

    @pl.when(i == 0)
    def _():
        issue(0, 0)

    @pl.when(i + 1 < n_steps)
    def _():
        issue(i + 1, 1 - slot)

    def drain(r, carry):
        _row_copy(h_hbm, 0, buf.at[slot], 0, sems.at[slot]).wait()
        return carry

    lax.fori_loop(0, bt, drain, 0, unroll=8)
    o_ref[...] = buf[slot].astype(o_ref.dtype)


def _gather_rows(h, tok, n_slots):
    d = h.shape[1]
    bt = _tile(n_slots, 256)
    n_steps = n_slots // bt
    return pl.pallas_call(
        functools.partial(_gather_rows_body, bt=bt, n_steps=n_steps),
        out_shape=jax.ShapeDtypeStruct((n_slots, d), BF16),
        grid_spec=pltpu.PrefetchScalarGridSpec(
            num_scalar_prefetch=1,
            grid=(n_steps,),
            in_specs=[pl.BlockSpec(memory_space=pl.ANY)],
            out_specs=pl.BlockSpec((bt, d), lambda i, tok: (i, 0)),
            scratch_shapes=[pltpu.VMEM((2, bt, d), F32), pltpu.SemaphoreType.DMA((2,))]),
        compiler_params=_params(("arbitrary",), _nbytes((bt, d), BF16),
                                scratch_bytes=_nbytes((2, bt, d), F32), temp_bytes=_nbytes((bt, d), F32)),
        name="moe_gather",
    )(tok, h)


def _refresh_expert_weights(te_ref, nt_ref, nx_ref, copies, stages, w_scrs, n_blocks):
    j, p = pl.program_id(0), pl.program_id(1)
    live = p < nt_ref[0]
    fresh = live & ((p == 0) | (te_ref[p] != te_ref[jnp.maximum(p - 1, 0)]))

    @pl.when(fresh)
    def _():
        @pl.when((j == 0) & (p == 0))
        def _():
            for copy in copies(te_ref[0], 0):
                copy.start()

        for copy, stage, w_scr in zip(copies(te_ref[p], j), stages, w_scrs):
            copy.wait()
            w_scr[...] = stage[...].astype(BF16)

        more = nx_ref[p] >= 0

        @pl.when(more)
        def _():
            for copy in copies(nx_ref[p], j):
                copy.start()

        @pl.when(jnp.logical_not(more) & (j + 1 < n_blocks))
        def _():
            for copy in copies(te_ref[0], j + 1):
                copy.start()

    return live


def _moe_up_body(te_ref, nt_ref, nx_ref, x_ref, w_hbm, o_ref, g_stage, u_stage, wg_scr, wu_scr, sems,
                 *, layer, bn, n_blocks):
    def copies(e, j):
        return (pltpu.make_async_copy(w_hbm.at[layer, e, :, pl.ds(j * bn, bn)], g_stage, sems.at[0]),
                pltpu.make_async_copy(w_hbm.at[layer, e, :, pl.ds((n_blocks + j) * bn, bn)], u_stage,
                                      sems.at[1]))

    live = _refresh_expert_weights(te_ref, nt_ref, nx_ref, copies, [g_stage, u_stage], [wg_scr, wu_scr],
                                   n_blocks)

    @pl.when(live)
    def _():
        o_ref[...] = _swiglu(x_ref[...], wg_scr[...], wu_scr[...]).astype(o_ref.dtype)

    @pl.when(jnp.logical_not(live))
    def _():
        o_ref[...] = jnp.zeros_like(o_ref)


def _moe_up(xs, w4, layer, tiles, bm, bn_pref=512):
    n_slots, d = xs.shape
    f = w4.shape[3] // 2
    bn = _tile(f, bn_pref)
    n_blocks = f // bn
    blocks = _nbytes((bm, d), xs.dtype) + _nbytes((bm, bn), BF16)
    scratch = 2 * (_nbytes((d, bn), F32) + _nbytes((d, bn), BF16))
    return pl.pallas_call(
        functools.partial(_moe_up_body, layer=layer, bn=bn, n_blocks=n_blocks),
        out_shape=jax.ShapeDtypeStruct((n_slots, f), BF16),
        grid_spec=pltpu.PrefetchScalarGridSpec(
            num_scalar_prefetch=3,
            grid=(n_blocks, n_slots // bm),
            in_specs=[pl.BlockSpec((bm, d), lambda j, p, te, nt, nx: (jnp.minimum(p, nt[0] - 1), 0)),
                      pl.BlockSpec(memory_space=pl.ANY)],
            out_specs=pl.BlockSpec((bm, bn), lambda j, p, te, nt, nx: (p, j)),
            scratch_shapes=[pltpu.VMEM((d, bn), F32), pltpu.VMEM((d, bn), F32),
                            pltpu.VMEM((d, bn), BF16), pltpu.VMEM((d, bn), BF16),
                            pltpu.SemaphoreType.DMA((2,))]),
        compiler_params=_params(("arbitrary", "arbitrary"), blocks, scratch_bytes=scratch,
                                temp_bytes=3 * _nbytes((bm, bn), F32)),
        name="moe_up",
    )(*tiles, xs, w4)


def _moe_down_body(te_ref, nt_ref, nx_ref, a_ref, w_hbm, o_ref, stage, w_scr, sem, *, layer, bn, n_blocks):
    def copies(e, j):
        return (pltpu.make_async_copy(w_hbm.at[layer, e, :, pl.ds(j * bn, bn)], stage, sem),)

    live = _refresh_expert_weights(te_ref, nt_ref, nx_ref, copies, [stage], [w_scr], n_blocks)

    @pl.when(live)
    def _():
        o_ref[...] = jnp.dot(a_ref[...], w_scr[...], preferred_element_type=F32)

    @pl.when(jnp.logical_not(live))
    def _():
        o_ref[...] = jnp.zeros_like(o_ref)


def _moe_down(act, w4, layer, tiles, bm, bn_pref=512):
    n_slots, f = act.shape
    d = w4.shape[3]
    bn = _tile(d, bn_pref)
    n_blocks = d // bn
    blocks = _nbytes((bm, f), BF16) + _nbytes((bm, bn), F32)
    scratch = _nbytes((f, bn), F32) + _nbytes((f, bn), BF16)
    return pl.pallas_call(
        functools.partial(_moe_down_body, layer=layer, bn=bn, n_blocks=n_blocks),
        out_shape=jax.ShapeDtypeStruct((n_slots, d), F32),
        grid_spec=pltpu.PrefetchScalarGridSpec(
            num_scalar_prefetch=3,
            grid=(n_blocks, n_slots // bm),
            in_specs=[pl.BlockSpec((bm, f), lambda j, p, te, nt, nx: (jnp.minimum(p, nt[0] - 1), 0)),
                      pl.BlockSpec(memory_space=pl.ANY)],
            out_specs=pl.BlockSpec((bm, bn), lambda j, p, te, nt, nx: (p, j)),
            scratch_shapes=[pltpu.VMEM((f, bn), F32), pltpu.VMEM((f, bn), BF16),
                            pltpu.SemaphoreType.DMA(())]),
        compiler_params=_params(("arbitrary", "arbitrary"), blocks, scratch_bytes=scratch,
                                temp_bytes=_nbytes((bm, bn), F32)),
        name="moe_down",
    )(*tiles, act, w4)


def _combine_body(pos1_ref, pos2_ref, r_ref, x_ref, ys_hbm, *rest, bt, n_steps, final_norm):
    if final_norm:
        g_ref, o_ref, buf, sems = rest
    else:
        o_ref, buf, sems = rest
    i = pl.program_id(0)
    slot = i % 2

    def issue(step, to_slot):
        def one(r, carry):
            t = step * bt + r
            _row_copy(ys_hbm, pos1_ref[t], buf.at[to_slot, 0], r, sems.at[to_slot]).start()
            _row_copy(ys_hbm, pos2_ref[t], buf.at[to_slot, 1], r, sems.at[to_slot]).start()
            return carry
        lax.fori_loop(0, bt, one, 0, unroll=8)

    @pl.when(i == 0)
    def _():
        issue(0, 0)

    @pl.when(i + 1 < n_steps)
    def _():
        issue(i + 1, 1 - slot)

    def drain(r, carry):
        _row_copy(ys_hbm, 0, buf.at[slot, 0], 0, sems.at[slot]).wait()
        _row_copy(ys_hbm, 0, buf.at[slot, 1], 0, sems.at[slot]).wait()
        return carry

    lax.fori_loop(0, bt, drain, 0, unroll=8)
    w1 = r_ref[:, _R_W1:_R_W1 + 1]
    w2 = r_ref[:, _R_W2:_R_W2 + 1]
    out = x_ref[...] + (w1 * buf[slot, 0] + w2 * buf[slot, 1])
    if final_norm:
        out = (out * lax.rsqrt(jnp.mean(out * out, axis=-1, keepdims=True) + EPS)) * g_ref[...]
    o_ref[...] = out


def _combine(x, ys, route, pos1, pos2, final_gain=None):
    n, d = x.shape
    bt = _tile(n, 256)
    n_steps = n // bt
    final_norm = final_gain is not None
    in_specs = [pl.BlockSpec((bt, LANES), lambda i, p1, p2: (i, 0)),
                pl.BlockSpec((bt, d), lambda i, p1, p2: (i, 0)),
                pl.BlockSpec(memory_space=pl.ANY)]
    args = [pos1, pos2, route, x, ys]
    if final_norm:
        in_specs.append(pl.BlockSpec((1, d), lambda i, p1, p2: (0, 0)))
        args.append(final_gain.astype(F32).reshape(1, d))
    return pl.pallas_call(
        functools.partial(_combine_body, bt=bt, n_steps=n_steps, final_norm=final_norm),
        out_shape=jax.ShapeDtypeStruct((n, d), F32),
        grid_spec=pltpu.PrefetchScalarGridSpec(
            num_scalar_prefetch=2,
            grid=(n_steps,),
            in_specs=in_specs,
            out_specs=pl.BlockSpec((bt, d), lambda i, p1, p2: (i, 0)),
            scratch_shapes=[pltpu.VMEM((2, TOP_K, bt, d), F32), pltpu.SemaphoreType.DMA((2,))]),
        compiler_params=_params(("arbitrary",), 2 * _nbytes((bt, d), F32),
                                scratch_bytes=_nbytes((2, TOP_K, bt, d), F32),
                                temp_bytes=2 * _nbytes((bt, d), F32)),
        name="moe_combine",
    )(*args)


def _moe(x, ln, router_w, router_b, w_gate_up, w_down, layer, final_gain, bm_pref=512):
    n, d = x.shape
    n_exp = router_w.shape[1]
    bm = _tile(n * TOP_K, bm_pref)
    n_tiles_max = (n * TOP_K) // bm + n_exp
    h_f32, logits = _rmsnorm_router(x, ln, router_w, router_b)
    route, counts = _route(logits, n_exp)
    counts = counts[0, :n_exp].astype(jnp.int32)
    sizes = (counts + bm - 1) // bm * bm
    ends = jnp.cumsum(sizes)
    starts = ends - sizes
    e1, e2 = route[:, _R_E1].astype(jnp.int32), route[:, _R_E2].astype(jnp.int32)
    pos1 = starts[e1] + route[:, _R_RANK1].astype(jnp.int32)
    pos2 = starts[e2] + route[:, _R_RANK2].astype(jnp.int32)
    n_tiles = (ends[-1] // bm).astype(jnp.int32).reshape(1)
    tile_start = jnp.arange(n_tiles_max, dtype=jnp.int32) * bm
    tile_expert = jnp.sum(tile_start[:, None] >= ends[None, :], axis=1).astype(jnp.int32)
    tile_expert = jnp.minimum(tile_expert, tile_expert[n_tiles[0] - 1])
    tile = jnp.arange(n_tiles_max, dtype=jnp.int32)
    later_other = ((tile[None, :] > tile[:, None]) & (tile[None, :] < n_tiles[0])
                   & (tile_expert[None, :] != tile_expert[:, None]))
    first_other = jnp.min(jnp.where(later_other, tile[None, :], n_tiles_max), axis=1)
    next_expert = jnp.where(first_other < n_tiles_max,
                            tile_expert[jnp.minimum(first_other, n_tiles_max - 1)], -1).astype(jnp.int32)
    tiles = (tile_expert, n_tiles, next_expert)

    n_slots = n_tiles_max * bm
    xs = _gather_rows(h_f32, _invert_slots(pos1, pos2, n_slots), n_slots)
    act = _moe_up(xs, w_gate_up, layer, tiles, bm)
    ys = _moe_down(act, w_down, layer, tiles, bm)
    return _combine(x, ys, route, pos1, pos2, final_gain)


def kernel(x, ln_mix, ln_ffn, w_in, conv_w, conv_b, i_bias, f_bias, mh_norm, p_conv, p_mlstm, w_out,
           ffn_w_gate_up, ffn_w_down, router_w, router_b, exp_w_gate_up, exp_w_down, final_norm):
    batch, seq, d = x.shape
    depth = ln_mix.shape[0]
    xf = x.reshape(batch * seq, d)
    wt_in = jnp.transpose(w_in, (0, 2, 1))
    normed = False
    for layer in range(depth):
        xf = _hybrid_mixer(xf, ln_mix[layer], wt_in, layer, conv_w[layer], conv_b[layer],
                           i_bias[layer], f_bias[layer], mh_norm[layer], p_conv, p_mlstm, w_out,
                           batch, seq)
        j = layer // 2
        if layer % 2 == 0:
            h = _rmsnorm(xf, ln_ffn[layer], BF16)
            act = _swiglu_up(h, ffn_w_gate_up, j)
            xf = _matmul_residual_ktiled(act, ffn_w_down[j].astype(BF16), xf)
        else:
            normed = layer == depth - 1
            xf = _moe(xf, ln_ffn[layer], router_w[j], router_b[j], exp_w_gate_up, exp_w_down, j,
                      final_norm if normed else None)
    if not normed:
        xf = _rmsnorm(xf, final_norm, x.dtype)
    return xf.reshape(batch, seq, d)
```

```python
import functools

import jax
import jax.numpy as jnp
from jax import lax
from jax.experimental import pallas as pl
from jax.experimental.pallas import tpu as pltpu

F32 = jnp.float32
BF16 = jnp.bfloat16

EPS = 1e-6
GATE_SOFTCAP = 15.0
N_HEADS = 8
TOP_K = 2

V7X_VMEM_BYTES = 64 * 2**20
LANES = 128
BF16_SUBLANES = 16
MASKED = -1e30

MLSTM_CHUNK = 256
HEAD_GROUP = 2
N_GATE_LANES = 2 * N_HEADS


def _tile(dim, pref):
    if dim <= pref:
        return dim
    t = pref
    while dim % t:
        t -= LANES
    return t


def _params(semantics, block_bytes, scratch_bytes=0, temp_bytes=0):
    need = 2 * block_bytes + scratch_bytes + temp_bytes + (4 << 20)
    return pltpu.CompilerParams(
        dimension_semantics=semantics,
        vmem_limit_bytes=int(min(max(need, 32 << 20), V7X_VMEM_BYTES - (6 << 20))))


def _nbytes(shape, dtype):
    n = 1
    for s in shape:
        n *= s
    return n * jnp.dtype(dtype).itemsize


def _rmsnorm_body(x_ref, g_ref, o_ref):
    x = x_ref[...]
    inv = lax.rsqrt(jnp.mean(x * x, axis=-1, keepdims=True) + EPS)
    o_ref[...] = ((x * inv) * g_ref[...]).astype(o_ref.dtype)


def _rmsnorm(x, g, out_dtype):
    n, d = x.shape
    bm = _tile(n, 256)
    return pl.pallas_call(
        _rmsnorm_body,
        out_shape=jax.ShapeDtypeStruct((n, d), out_dtype),
        grid=(n // bm,),
        in_specs=[pl.BlockSpec((bm, d), lambda i: (i, 0)),
                  pl.BlockSpec((1, d), lambda i: (0, 0))],
        out_specs=pl.BlockSpec((bm, d), lambda i: (i, 0)),
        compiler_params=_params(("parallel",), _nbytes((bm, d), F32) * 2),
        name="rmsnorm",
    )(x, g.reshape(1, d))


def _rmsnorm_router_body(x_ref, g_ref, whi_ref, wlo_ref, rb_ref, hf_ref, lg_ref):
    x = x_ref[...]
    inv = lax.rsqrt(jnp.mean(x * x, axis=-1, keepdims=True) + EPS)
    h = (x * inv) * g_ref[...]
    h_hi = h.astype(BF16)
    hf_ref[...] = h
    h_lo = (h - h_hi.astype(F32)).astype(BF16)
    lg = (jnp.dot(h_hi, whi_ref[...], preferred_element_type=F32)
          + jnp.dot(h_lo, whi_ref[...], preferred_element_type=F32)
          + jnp.dot(h_hi, wlo_ref[...], preferred_element_type=F32))
    lg_ref[...] = lg + rb_ref[...]


def _rmsnorm_router(x, g, router_w, router_b):
    n, d = x.shape
    n_exp = router_w.shape[1]
    bm = _tile(n, 256)
    w = jnp.pad(router_w.astype(F32), ((0, 0), (0, LANES - n_exp)))
    w_hi = w.astype(BF16)
    w_lo = (w - w_hi.astype(F32)).astype(BF16)
    rb = jnp.pad(router_b.astype(F32), (0, LANES - n_exp)).reshape(1, LANES)
    row = lambda i: (i, 0)
    fixed = lambda i: (0, 0)
    return pl.pallas_call(
        _rmsnorm_router_body,
        out_shape=(jax.ShapeDtypeStruct((n, d), F32), jax.ShapeDtypeStruct((n, LANES), F32)),
        grid=(n // bm,),
        in_specs=[pl.BlockSpec((bm, d), row), pl.BlockSpec((1, d), fixed),
                  pl.BlockSpec((d, LANES), fixed), pl.BlockSpec((d, LANES), fixed),
                  pl.BlockSpec((1, LANES), fixed)],
        out_specs=(pl.BlockSpec((bm, d), row), pl.BlockSpec((bm, LANES), row)),
        compiler_params=_params(("parallel",), _nbytes((bm, d), F32) * 2 + _nbytes((d, LANES), BF16) * 2),
        name="rmsnorm_router",
    )(x, g.reshape(1, d), w_hi, w_lo, rb)


def _refresh_weights(copies, stages, w_scrs, n_blocks):
    j = pl.program_id(0)

    @pl.when(pl.program_id(1) == 0)
    def _():
        @pl.when(j == 0)
        def _():
            for copy in copies:
                copy(0).start()

        for copy, stage, w_scr in zip(copies, stages, w_scrs):
            copy(j).wait()
            w_scr[...] = stage[...].astype(BF16)

        @pl.when(j + 1 < n_blocks)
        def _():
            for copy in copies:
                copy(j + 1).start()


def _in_proj_body(x_ref, wt_hbm, o_ref, stage, w_scr, sem, *, layer, bn, n_blocks, n_main_blocks):
    def copy(j):
        row0 = j * bn + jnp.where(j >= n_main_blocks, N_GATE_LANES, 0)
        return pltpu.make_async_copy(wt_hbm.at[layer, pl.ds(row0, bn)], stage, sem)

    _refresh_weights([copy], [stage], [w_scr], n_blocks)
    o_ref[...] = lax.dot_general(x_ref[...], w_scr[...], (((1,), (1,)), ((), ())),
                                 preferred_element_type=F32).astype(o_ref.dtype)


def _in_proj(x, wt, layer, d, bm_pref=1024, bn_pref=1024):
    m, k = x.shape
    dc = d // 2
    bm, bn = _tile(m, bm_pref), _tile(dc, bn_pref)
    nb = dc // bn
    n_units = _N_MAIN_UNITS + 4
    n_blocks = n_units * nb
    out_block = lambda j: jnp.where(j < 3 * nb, j + 10 * nb, jnp.where(j < 5 * nb, j + 5 * nb, j - 5 * nb))
    blocks = _nbytes((bm, k), x.dtype) + _nbytes((bm, bn), BF16)
    scratch = _nbytes((bn, k), F32) + _nbytes((bn, k), BF16)
    return pl.pallas_call(
        functools.partial(_in_proj_body, layer=layer, bn=bn, n_blocks=n_blocks,
                          n_main_blocks=_N_MAIN_UNITS * nb),
        out_shape=jax.ShapeDtypeStruct((m, n_units * dc), BF16),
        grid=(n_blocks, m // bm),
        in_specs=[pl.BlockSpec((bm, k), lambda j, i: (i, 0)),
                  pl.BlockSpec(memory_space=pl.ANY)],
        out_specs=pl.BlockSpec((bm, bn), lambda j, i: (i, out_block(j))),
        scratch_shapes=[pltpu.VMEM((bn, k), F32), pltpu.VMEM((bn, k), BF16), pltpu.SemaphoreType.DMA(())],
        compiler_params=_params(("arbitrary", "arbitrary"), blocks, scratch_bytes=scratch,
                                temp_bytes=_nbytes((bm, bn), F32)),
        name="in_proj",
    )(x, wt)


def _matmul_residual_body(a_ref, w_hbm, x_ref, o_ref, stage, w_scr, sem, *, layer, bn, n_blocks):
    def copy(j):
        return pltpu.make_async_copy(w_hbm.at[layer, :, pl.ds(j * bn, bn)], stage, sem)

    _refresh_weights([copy], [stage], [w_scr], n_blocks)
    o_ref[...] = x_ref[...] + jnp.dot(a_ref[...], w_scr[...], preferred_element_type=F32)


def _matmul_residual(a, w3, layer, x, bm_pref=1024, bn_pref=512):
    m, k = a.shape
    n = w3.shape[2]
    bm, bn = _tile(m, bm_pref), _tile(n, bn_pref)
    n_blocks = n // bn
    blocks = _nbytes((bm, k), a.dtype) + 2 * _nbytes((bm, bn), F32)
    scratch = _nbytes((k, bn), F32) + _nbytes((k, bn), BF16)
    return pl.pallas_call(
        functools.partial(_matmul_residual_body, layer=layer, bn=bn, n_blocks=n_blocks),
        out_shape=jax.ShapeDtypeStruct((m, n), F32),
        grid=(n_blocks, m // bm),
        in_specs=[pl.BlockSpec((bm, k), lambda j, i: (i, 0)),
                  pl.BlockSpec(memory_space=pl.ANY),
                  pl.BlockSpec((bm, bn), lambda j, i: (i, j))],
        out_specs=pl.BlockSpec((bm, bn), lambda j, i: (i, j)),
        scratch_shapes=[pltpu.VMEM((k, bn), F32), pltpu.VMEM((k, bn), BF16), pltpu.SemaphoreType.DMA(())],
        compiler_params=_params(("arbitrary", "arbitrary"), blocks, scratch_bytes=scratch,
                                temp_bytes=_nbytes((bm, bn), F32)),
        name="out_proj",
    )(a, w3, x)


def _matmul_residual_ktiled_body(a_ref, w_ref, x_ref, o_ref):
    @pl.when(pl.program_id(2) == 0)
    def _():
        o_ref[...] = x_ref[...]

    o_ref[...] += jnp.dot(a_ref[...], w_ref[...], preferred_element_type=F32)


def _matmul_residual_ktiled(a, w, x, bm_pref=1024, bn_pref=1024, bk_pref=3584):
    m, k = a.shape
    n = w.shape[1]
    bm, bn, bk = _tile(m, bm_pref), _tile(n, bn_pref), _tile(k, bk_pref)
    blocks = _nbytes((bm, bk), a.dtype) + _nbytes((bk, bn), w.dtype) + 2 * _nbytes((bm, bn), F32)
    return pl.pallas_call(
        _matmul_residual_ktiled_body,
        out_shape=jax.ShapeDtypeStruct((m, n), F32),
        grid=(m // bm, n // bn, k // bk),
        in_specs=[pl.BlockSpec((bm, bk), lambda i, j, kk: (i, kk)),
                  pl.BlockSpec((bk, bn), lambda i, j, kk: (kk, j)),
                  pl.BlockSpec((bm, bn), lambda i, j, kk: (i, j))],
        out_specs=pl.BlockSpec((bm, bn), lambda i, j, kk: (i, j)),
        compiler_params=_params(("parallel", "parallel", "arbitrary"), blocks,
                                temp_bytes=_nbytes((bm, bn), F32)),
        name="ffn_down",
    )(a, w, x)


def _swiglu(h, wg, wu):
    g = jnp.dot(h, wg, preferred_element_type=F32)
    u = jnp.dot(h, wu, preferred_element_type=F32)
    return g * jax.nn.sigmoid(g) * u


def _swiglu_up_body(h_ref, w_hbm, o_ref, g_stage, u_stage, wg_scr, wu_scr, sems, *, layer, bn, n_blocks):
    def copy_gate(j):
        return pltpu.make_async_copy(w_hbm.at[layer, :, pl.ds(j * bn, bn)], g_stage, sems.at[0])

    def copy_up(j):
        return pltpu.make_async_copy(w_hbm.at[layer, :, pl.ds((n_blocks + j) * bn, bn)], u_stage, sems.at[1])

    _refresh_weights([copy_gate, copy_up], [g_stage, u_stage], [wg_scr, wu_scr], n_blocks)
    o_ref[...] = _swiglu(h_ref[...], wg_scr[...], wu_scr[...]).astype(o_ref.dtype)


def _swiglu_up(h, w3, layer, bm_pref=1024, bn_pref=512):
    m, k = h.shape
    f = w3.shape[2] // 2
    bm, bn = _tile(m, bm_pref), _tile(f, bn_pref)
    n_blocks = f // bn
    blocks = _nbytes((bm, k), h.dtype) + _nbytes((bm, bn), BF16)
    scratch = 2 * (_nbytes((k, bn), F32) + _nbytes((k, bn), BF16))
    return pl.pallas_call(
        functools.partial(_swiglu_up_body, layer=layer, bn=bn, n_blocks=n_blocks),
        out_shape=jax.ShapeDtypeStruct((m, f), BF16),
        grid=(n_blocks, m // bm),
        in_specs=[pl.BlockSpec((bm, k), lambda j, i: (i, 0)),
                  pl.BlockSpec(memory_space=pl.ANY)],
        out_specs=pl.BlockSpec((bm, bn), lambda j, i: (i, j)),
        scratch_shapes=[pltpu.VMEM((k, bn), F32), pltpu.VMEM((k, bn), F32),
                        pltpu.VMEM((k, bn), BF16), pltpu.VMEM((k, bn), BF16),
                        pltpu.SemaphoreType.DMA((2,))],
        compiler_params=_params(("arbitrary", "arbitrary"), blocks, scratch_bytes=scratch,
                                temp_bytes=3 * _nbytes((bm, bn), F32)),
        name="ffn_up",
    )(h, w3)


_COL_V, _COL_O, _COL_GA, _COL_GB = 0, 1, 2, 3
_COL_Q, _COL_K, _COL_BG, _COL_CG, _COL_VC = 8, 9, 10, 11, 12
_N_MAIN_UNITS = 9


def _gates_body(x_ref, wt_hbm, b_ref, gc_ref, gr_ref, stage, w_scr, sem, *, layer, row0):
    @pl.when(pl.program_id(0) == 0)
    def _():
        stage[...] = jnp.zeros_like(stage)
        rows = pltpu.make_async_copy(wt_hbm.at[layer, pl.ds(row0, N_GATE_LANES)],
                                     stage.at[pl.ds(0, N_GATE_LANES)], sem)
        rows.start()
        rows.wait()
        w_scr[...] = stage[...].astype(BF16)

    z = lax.dot_general(x_ref[...], w_scr[...], (((1,), (1,)), ((), ())),
                        preferred_element_type=F32) + b_ref[...]
    sc = GATE_SOFTCAP * jnp.tanh(z / GATE_SOFTCAP)
    logf = jnp.minimum(sc, 0.0) - jnp.log1p(jnp.exp(-jnp.abs(sc)))
    rows = z.shape[0]
    row = lax.broadcasted_iota(jnp.int32, z.shape, 0)
    cum = logf
    shift = 1
    while shift < rows:
        cum = cum + jnp.where(row >= shift, pltpu.roll(cum, shift, 0), 0.0)
        shift *= 2
    lane = lax.broadcasted_iota(jnp.int32, z.shape, 1)
    gc = jnp.where(lane < N_HEADS, sc, cum)
    gc_ref[...] = gc
    gr_ref[0] = gc.T[:N_GATE_LANES, :]


def _gates(xn, wt, layer, i_bias, f_bias, chunk):
    n, d = xn.shape
    bias = jnp.pad(jnp.concatenate([i_bias, f_bias]).astype(F32), (0, LANES - N_GATE_LANES))
    return pl.pallas_call(
        functools.partial(_gates_body, layer=layer, row0=_N_MAIN_UNITS * (d // 2)),
        out_shape=(jax.ShapeDtypeStruct((n, LANES), F32),
                   jax.ShapeDtypeStruct((n // chunk, N_GATE_LANES, chunk), F32)),
        grid=(n // chunk,),
        in_specs=[pl.BlockSpec((chunk, d), lambda i: (i, 0)),
                  pl.BlockSpec(memory_space=pl.ANY),
                  pl.BlockSpec((1, LANES), lambda i: (0, 0))],
        out_specs=(pl.BlockSpec((chunk, LANES), lambda i: (i, 0)),
                   pl.BlockSpec((1, N_GATE_LANES, chunk), lambda i: (i, 0, 0))),
        scratch_shapes=[pltpu.VMEM((LANES, d), F32), pltpu.VMEM((LANES, d), BF16),
                        pltpu.SemaphoreType.DMA(())],
        compiler_params=_params(("arbitrary",), _nbytes((chunk, d), BF16),
                                scratch_bytes=_nbytes((LANES, d), F32) + _nbytes((LANES, d), BF16)),
        name="mlstm_gates",
    )(xn, wt, bias.reshape(1, LANES))


def _wide(col, n):
    return col if n == LANES else jnp.concatenate([col] * (n // LANES), axis=1)


def _mlstm_body(q_ref, k_ref, v_ref, o_ref, gc_ref, gr_ref, g_ref, y_ref, c_scr, n_scr, m_scr,
                *, dqk, dv):
    @pl.when(pl.program_id(1) == 0)
    def _():
        c_scr[...] = jnp.zeros_like(c_scr)
        n_scr[...] = jnp.zeros_like(n_scr)
        m_scr[...] = jnp.zeros_like(m_scr)

    chunk = q_ref.shape[0]
    t_idx = lax.broadcasted_iota(jnp.int32, (chunk, chunk), 0)
    s_idx = lax.broadcasted_iota(jnp.int32, (chunk, chunk), 1)
    causal = s_idx <= t_idx
    gc = gc_ref[...]
    gr = gr_ref[0]
    scale = dqk ** -0.5
    for h0 in range(0, N_HEADS, HEAD_GROUP):
        heads = range(h0, h0 + HEAD_GROUP)
        st = {h: {} for h in heads}
        for h in heads:
            t = st[h]
            t["q"] = q_ref[:, h * dqk:(h + 1) * dqk]
            t["kf"] = k_ref[:, h * dqk:(h + 1) * dqk].astype(F32) * scale
            t["v"] = v_ref[:, h * dv:(h + 1) * dv]
            t["i_col"] = jnp.broadcast_to(gc[:, h:h + 1], (chunk, LANES))
            t["b_col"] = jnp.broadcast_to(gc[:, N_HEADS + h:N_HEADS + h + 1], (chunk, LANES))
            i_row, b_row = gr[h:h + 1, :], gr[N_HEADS + h:N_HEADS + h + 1, :]
            t["m_prev"] = m_scr[h]
            t["d_log"] = jnp.where(causal, _wide(t["b_col"], chunk) - b_row + i_row, MASKED)
            t["inter"] = t["b_col"] + t["m_prev"]
        for h in heads:
            t = st[h]
            t["m_t"] = jnp.maximum(t["inter"], jnp.max(t["d_log"], axis=-1, keepdims=True))
            t["w_inter"] = jnp.exp(t["inter"] - t["m_t"])
        for h in heads:
            t = st[h]
            t["s_qk"] = (lax.dot_general(t["q"], t["kf"].astype(BF16), (((1,), (1,)), ((), ())),
                                         preferred_element_type=F32)
                         * jnp.exp(t["d_log"] - _wide(t["m_t"], chunk)))
        for h in heads:
            t = st[h]
            t["c_mat"] = c_scr[h]
            t["n_vec"] = n_scr[h]
            t["num"] = (_wide(t["w_inter"], dv)
                        * jnp.dot(t["q"], t["c_mat"].astype(BF16), preferred_element_type=F32)
                        + jnp.dot(t["s_qk"].astype(BF16), t["v"], preferred_element_type=F32))
            t["den"] = (t["w_inter"] * jnp.sum(t["q"].astype(F32) * t["n_vec"], axis=-1, keepdims=True)
                        + jnp.sum(t["s_qk"], axis=-1, keepdims=True))
        for h in heads:
            t = st[h]
            m_new = t["m_t"][chunk - 1:chunk, :1]
            b_last = t["b_col"][chunk - 1:chunk, :1]
            decay = jnp.exp(b_last + t["m_prev"] - m_new)
            kw = t["kf"] * _wide(jnp.exp(b_last - t["b_col"] + t["i_col"] - m_new), dqk)
            c_scr[h] = decay * t["c_mat"] + lax.dot_general(kw.astype(BF16), t["v"], (((0,), (0,)), ((), ())),
                                                            preferred_element_type=F32)
            n_scr[h] = decay * t["n_vec"] + jnp.sum(kw, axis=0, keepdims=True)
            m_scr[h] = m_new
        for h in heads:
            t = st[h]
            hid = t["num"] * _wide(1.0 / jnp.maximum(jnp.abs(t["den"]), jnp.exp(-t["m_t"])), dv)
            hid = hid * lax.rsqrt(jnp.mean(hid * hid, axis=-1, keepdims=True) + EPS)
            hid = hid * g_ref[:, h * dv:(h + 1) * dv]
            gate = jax.nn.sigmoid(o_ref[:, h * dv:(h + 1) * dv].astype(F32))
            y_ref[:, h * dv:(h + 1) * dv] = (gate * hid).astype(y_ref.dtype)


def _mlstm(y, gc, gr, mh_norm, batch, seq, d, chunk):
    n = batch * seq
    nc = seq // chunk
    dqk, dv = d // 2 // N_HEADS, d // N_HEADS
    rows = lambda col: (lambda b, c: (b * nc + c, col))
    blocks = (2 * _nbytes((chunk, d // 2), BF16) + 3 * _nbytes((chunk, d), BF16)
              + _nbytes((chunk, LANES), F32))
    state = _nbytes((N_HEADS, dqk, dv), F32)
    return pl.pallas_call(
        functools.partial(_mlstm_body, dqk=dqk, dv=dv),
        out_shape=jax.ShapeDtypeStruct((n, d), BF16),
        grid=(batch, nc),
        in_specs=[pl.BlockSpec((chunk, d // 2), rows(_COL_Q)),
                  pl.BlockSpec((chunk, d // 2), rows(_COL_K)),
                  pl.BlockSpec((chunk, d), rows(_COL_V)),
                  pl.BlockSpec((chunk, d), rows(_COL_O)),
                  pl.BlockSpec((chunk, LANES), rows(0)),
                  pl.BlockSpec((1, N_GATE_LANES, chunk), lambda b, c: (b * nc + c, 0, 0)),
                  pl.BlockSpec((1, d), lambda b, c: (0, 0))],
        out_specs=pl.BlockSpec((chunk, d), rows(0)),
        scratch_shapes=[pltpu.VMEM((N_HEADS, dqk, dv), F32),
                        pltpu.VMEM((N_HEADS, 1, dqk), F32),
                        pltpu.VMEM((N_HEADS, 1, 1), F32)],
        compiler_params=_params(("parallel", "arbitrary"), blocks, scratch_bytes=state,
                                temp_bytes=8 << 20),
        name="mlstm",
    )(y, y, y, y, gc, gr, mh_norm.astype(F32).reshape(1, d))


def _conv_body(bg_ref, cg_ref, vc_ref, cgp_ref, vcp_ref, w_ref, b_ref, o_ref, *, tiles_per_seq):
    u = cg_ref[...].astype(F32) * vc_ref[...].astype(F32)
    keep = (pl.program_id(0) % tiles_per_seq != 0).astype(F32)
    halo = cgp_ref[...].astype(F32) * vcp_ref[...].astype(F32) * keep
    prev1 = halo[BF16_SUBLANES - 1:BF16_SUBLANES, :]
    prev2 = halo[BF16_SUBLANES - 2:BF16_SUBLANES - 1, :]
    row = lax.broadcasted_iota(jnp.int32, u.shape, 0)
    u1 = jnp.where(row == 0, prev1, pltpu.roll(u, 1, 0))
    u2 = jnp.where(row == 0, prev2, jnp.where(row == 1, prev1, pltpu.roll(u, 2, 0)))
    w = w_ref[...]
    conv = w[0:1, :] * u2 + w[1:2, :] * u1 + w[2:3, :] * u + b_ref[...]
    o_ref[...] = (bg_ref[...].astype(F32) * conv).astype(o_ref.dtype)


def _gated_conv(y, conv_w, conv_b, seq, d):
    n = y.shape[0]
    dc = d // 2
    bm, bc = _tile(seq, 512), _tile(dc, 1024)
    ncol = dc // bc
    halo_rows = bm // BF16_SUBLANES
    cur = lambda col: (lambda i, j: (i, col * ncol + j))
    prev = lambda col: (lambda i, j: (jnp.maximum(i * halo_rows - 1, 0), col * ncol + j))
    return pl.pallas_call(
        functools.partial(_conv_body, tiles_per_seq=seq // bm),
        out_shape=jax.ShapeDtypeStruct((n, dc), BF16),
        grid=(n // bm, ncol),
        in_specs=[pl.BlockSpec((bm, bc), cur(_COL_BG)),
                  pl.BlockSpec((bm, bc), cur(_COL_CG)),
                  pl.BlockSpec((bm, bc), cur(_COL_VC)),
                  pl.BlockSpec((BF16_SUBLANES, bc), prev(_COL_CG)),
                  pl.BlockSpec((BF16_SUBLANES, bc), prev(_COL_VC)),
                  pl.BlockSpec((conv_w.shape[0], bc), lambda i, j: (0, j)),
                  pl.BlockSpec((1, bc), lambda i, j: (0, j))],
        out_specs=pl.BlockSpec((bm, bc), lambda i, j: (i, j)),
        compiler_params=_params(("parallel", "parallel"), 4 * _nbytes((bm, bc), BF16),
                                temp_bytes=8 * _nbytes((bm, bc), F32)),
        name="gated_conv",
    )(y, y, y, y, y, conv_w.astype(F32), conv_b.astype(F32).reshape(1, dc))


def _merge_body(ya_ref, yb_ref, ga_ref, gb_ref, pc_hbm, pm_hbm, o_ref, pc_stage, pm_stage, pc_scr, pm_scr,
                sems, *, layer, bn, n_blocks):
    def copy_conv(j):
        return pltpu.make_async_copy(pc_hbm.at[layer, :, pl.ds(j * bn, bn)], pc_stage, sems.at[0])

    def copy_mlstm(j):
        return pltpu.make_async_copy(pm_hbm.at[layer, :, pl.ds(j * bn, bn)], pm_stage, sems.at[1])

    _refresh_weights([copy_conv, copy_mlstm], [pc_stage, pm_stage], [pc_scr, pm_scr], n_blocks)
    a = jnp.dot(ya_ref[...], pc_scr[...], preferred_element_type=F32)
    b = jnp.dot(yb_ref[...], pm_scr[...], preferred_element_type=F32)
    merged = (jax.nn.sigmoid(ga_ref[...].astype(F32)) * a
              + jax.nn.sigmoid(gb_ref[...].astype(F32)) * b)
    o_ref[...] = merged.astype(o_ref.dtype)


def _merge(y_a, y_b, p_conv3, p_mlstm3, layer, y, d, bm_pref=1024, bn_pref=512):
    n = y_a.shape[0]
    bm, bn = _tile(n, bm_pref), _tile(d, bn_pref)
    n_blocks = d // bn
    blocks = _nbytes((bm, d // 2), BF16) + _nbytes((bm, d), BF16) + 3 * _nbytes((bm, bn), BF16)
    scratch = (_nbytes((d // 2, bn), F32) + _nbytes((d, bn), F32)
               + _nbytes((d // 2, bn), BF16) + _nbytes((d, bn), BF16))
    return pl.pallas_call(
        functools.partial(_merge_body, layer=layer, bn=bn, n_blocks=n_blocks),
        out_shape=jax.ShapeDtypeStruct((n, d), BF16),
        grid=(n_blocks, n // bm),
        in_specs=[pl.BlockSpec((bm, d // 2), lambda j, i: (i, 0)),
                  pl.BlockSpec((bm, d), lambda j, i: (i, 0)),
                  pl.BlockSpec((bm, bn), lambda j, i: (i, _COL_GA * n_blocks + j)),
                  pl.BlockSpec((bm, bn), lambda j, i: (i, _COL_GB * n_blocks + j)),
                  pl.BlockSpec(memory_space=pl.ANY),
                  pl.BlockSpec(memory_space=pl.ANY)],
        out_specs=pl.BlockSpec((bm, bn), lambda j, i: (i, j)),
        scratch_shapes=[pltpu.VMEM((d // 2, bn), F32), pltpu.VMEM((d, bn), F32),
                        pltpu.VMEM((d // 2, bn), BF16), pltpu.VMEM((d, bn), BF16),
                        pltpu.SemaphoreType.DMA((2,))],
        compiler_params=_params(("arbitrary", "arbitrary"), blocks, scratch_bytes=scratch,
                                temp_bytes=3 * _nbytes((bm, bn), F32)),
        name="merge",
    )(y_a, y_b, y, y, p_conv3, p_mlstm3)


def _hybrid_mixer(x, ln, wt_in, layer, conv_w, conv_b, i_bias, f_bias, mh_norm, p_conv, p_mlstm, w_out,
                  batch, seq):
    n, d = x.shape
    chunk = _tile(seq, MLSTM_CHUNK)
    xn = _rmsnorm(x, ln, BF16)
    y = _in_proj(xn, wt_in, layer, d)
    gc, gr = _gates(xn, wt_in, layer, i_bias, f_bias, chunk)
    y_b = _mlstm(y, gc, gr, mh_norm, batch, seq, d, chunk)
    y_a = _gated_conv(y, conv_w, conv_b, seq, d)
    merged = _merge(y_a, y_b, p_conv, p_mlstm, layer, y, d)
    return _matmul_residual(merged, w_out, layer, x)


_R_E1, _R_E2, _R_RANK1, _R_RANK2, _R_W1, _R_W2 = 0, 1, 2, 3, 4, 5


def _route_body(lg_ref, r_ref, cnt_ref, carry, *, n_exp):
    @pl.when(pl.program_id(0) == 0)
    def _():
        carry[...] = jnp.zeros_like(carry)

    lg = lg_ref[...]
    bt = lg.shape[0]
    lane_i = lax.broadcasted_iota(jnp.int32, lg.shape, 1)
    lane = lane_i.astype(F32)
    x1 = jnp.where(lane_i < n_exp, lg, MASKED)
    m1 = jnp.max(x1, axis=-1, keepdims=True)
    e1 = jnp.min(jnp.where(x1 == m1, lane, float(LANES)), axis=-1, keepdims=True)
    x2 = jnp.where(lane == e1, MASKED, x1)
    m2 = jnp.max(x2, axis=-1, keepdims=True)
    e2 = jnp.min(jnp.where(x2 == m2, lane, float(LANES)), axis=-1, keepdims=True)
    ex = jnp.exp(m2 - m1)
    w1 = 1.0 / (1.0 + ex)
    w2 = ex / (1.0 + ex)
    hit1, hit2 = lane == e1, lane == e2
    onehot = (hit1 | hit2).astype(F32)
    t_idx = lax.broadcasted_iota(jnp.int32, (bt, bt), 0)
    s_idx = lax.broadcasted_iota(jnp.int32, (bt, bt), 1)
    earlier = (s_idx < t_idx).astype(BF16)
    rank = jnp.dot(earlier, onehot.astype(BF16), preferred_element_type=F32) + carry[...]
    rank1 = jnp.sum(jnp.where(hit1, rank, 0.0), axis=-1, keepdims=True)
    rank2 = jnp.sum(jnp.where(hit2, rank, 0.0), axis=-1, keepdims=True)
    carry[...] += jnp.sum(onehot, axis=0, keepdims=True)
    rec = jnp.zeros_like(lg)
    for idx, val in ((_R_E1, e1), (_R_E2, e2), (_R_RANK1, rank1), (_R_RANK2, rank2),
                     (_R_W1, w1), (_R_W2, w2)):
        rec = jnp.where(lane_i == idx, val, rec)
    r_ref[...] = rec
    cnt_ref[...] = jnp.broadcast_to(carry[...], cnt_ref.shape)


def _route(logits, n_exp):
    n = logits.shape[0]
    bt = _tile(n, 512)
    return pl.pallas_call(
        functools.partial(_route_body, n_exp=n_exp),
        out_shape=(jax.ShapeDtypeStruct((n, LANES), F32), jax.ShapeDtypeStruct((8, LANES), F32)),
        grid=(n // bt,),
        in_specs=[pl.BlockSpec((bt, LANES), lambda i: (i, 0))],
        out_specs=(pl.BlockSpec((bt, LANES), lambda i: (i, 0)),
                   pl.BlockSpec((8, LANES), lambda i: (0, 0))),
        scratch_shapes=[pltpu.VMEM((1, LANES), F32)],
        compiler_params=_params(("arbitrary",), 2 * _nbytes((bt, LANES), F32),
                                temp_bytes=_nbytes((bt, bt), F32)),
        name="moe_route",
    )(logits)


def _row_copy(src_hbm, src_row, dst_ref, dst_row, sem):
    return pltpu.make_async_copy(src_hbm.at[pl.ds(src_row, 1)], dst_ref.at[pl.ds(dst_row, 1)], sem)


def _invert_slots_body(pos1_ref, pos2_ref, tok_ref, *, n, n_slots):
    def clear(s, carry):
        tok_ref[s] = 0
        return carry

    def put(t, carry):
        tok_ref[pos1_ref[t]] = t
        tok_ref[pos2_ref[t]] = t
        return carry

    lax.fori_loop(0, n_slots, clear, 0, unroll=8)
    lax.fori_loop(0, n, put, 0, unroll=8)


def _invert_slots(pos1, pos2, n_slots):
    smem = pl.BlockSpec(memory_space=pltpu.SMEM)
    return pl.pallas_call(
        functools.partial(_invert_slots_body, n=pos1.shape[0], n_slots=n_slots),
        out_shape=jax.ShapeDtypeStruct((n_slots,), jnp.int32),
        in_specs=[smem, smem],
        out_specs=smem,
        name="moe_invert",
    )(pos1, pos2)


def _gather_rows_body(tok_ref, h_hbm, o_ref, buf, sems, *, bt, n_steps):
    i = pl.program_id(0)
    slot = i % 2

    def issue(step, to_slot):
        def one(r, carry):
            _row_copy(h_hbm, tok_ref[step * bt + r], buf.at[to_slot], r, sems.at[to_slot]).start()
            return carry
        lax.fori_loop(0, bt, one, 0, unroll=8)

    @pl.when(i == 0)
    def _():
        issue(0, 0)

    @pl.when(i + 1 < n_steps)
    def _():
        issue(i + 1, 1 - slot)

    def drain(r, carry):
        _row_copy(h_hbm, 0, buf.at[slot], 0, sems.at[slot]).wait()
        return carry

    lax.fori_loop(0, bt, drain, 0, unroll=8)
    o_ref[...] = buf[slot].astype(o_ref.dtype)


def _gather_rows(h, tok, n_slots):
    d = h.shape[1]
    bt = _tile(n_slots, 256)
    n_steps = n_slots // bt
    return pl.pallas_call(
        functools.partial(_gather_rows_body, bt=bt, n_steps=n_steps),
        out_shape=jax.ShapeDtypeStruct((n_slots, d), BF16),
        grid_spec=pltpu.PrefetchScalarGridSpec(
            num_scalar_prefetch=1,
            grid=(n_steps,),
            in_specs=[pl.BlockSpec(memory_space=pl.ANY)],
            out_specs=pl.BlockSpec((bt, d), lambda i, tok: (i, 0)),
            scratch_shapes=[pltpu.VMEM((2, bt, d), F32), pltpu.SemaphoreType.DMA((2,))]),
        compiler_params=_params(("arbitrary",), _nbytes((bt, d), BF16),
                                scratch_bytes=_nbytes((2, bt, d), F32), temp_bytes=_nbytes((bt, d), F32)),
        name="moe_gather",
    )(tok, h)


def _refresh_expert_weights(te_ref, nt_ref, nx_ref, copies, stages, w_scrs, n_blocks):
    j, p = pl.program_id(0), pl.program_id(1)
    live = p < nt_ref[0]
    fresh = live & ((p == 0) | (te_ref[p] != te_ref[jnp.maximum(p - 1, 0)]))

    @pl.when(fresh)
    def _():
        @pl.when((j == 0) & (p == 0))
        def _():
            for copy in copies(te_ref[0], 0):
                copy.start()

        for copy, stage, w_scr in zip(copies(te_ref[p], j), stages, w_scrs):
            copy.wait()
            w_scr[...] = stage[...].astype(BF16)

        more = nx_ref[p] >= 0

        @pl.when(more)
        def _():
            for copy in copies(nx_ref[p], j):
                copy.start()

        @pl.when(jnp.logical_not(more) & (j + 1 < n_blocks))
        def _():
            for copy in copies(te_ref[0], j + 1):
                copy.start()

    return live


def _moe_up_body(te_ref, nt_ref, nx_ref, x_ref, w_hbm, o_ref, g_stage, u_stage, wg_scr, wu_scr, sems,
                 *, layer, bn, n_blocks):
    def copies(e, j):
        return (pltpu.make_async_copy(w_hbm.at[layer, e, :, pl.ds(j * bn, bn)], g_stage, sems.at[0]),
                pltpu.make_async_copy(w_hbm.at[layer, e, :, pl.ds((n_blocks + j) * bn, bn)], u_stage,
                                      sems.at[1]))

    live = _refresh_expert_weights(te_ref, nt_ref, nx_ref, copies, [g_stage, u_stage], [wg_scr, wu_scr],
                                   n_blocks)

    @pl.when(live)
    def _():
        o_ref[...] = _swiglu(x_ref[...], wg_scr[...], wu_scr[...]).astype(o_ref.dtype)

    @pl.when(jnp.logical_not(live))
    def _():
        o_ref[...] = jnp.zeros_like(o_ref)


def _moe_up(xs, w4, layer, tiles, bm, bn_pref=512):
    n_slots, d = xs.shape
    f = w4.shape[3] // 2
    bn = _tile(f, bn_pref)
    n_blocks = f // bn
    blocks = _nbytes((bm, d), xs.dtype) + _nbytes((bm, bn), BF16)
    scratch = 2 * (_nbytes((d, bn), F32) + _nbytes((d, bn), BF16))
    return pl.pallas_call(
        functools.partial(_moe_up_body, layer=layer, bn=bn, n_blocks=n_blocks),
        out_shape=jax.ShapeDtypeStruct((n_slots, f), BF16),
        grid_spec=pltpu.PrefetchScalarGridSpec(
            num_scalar_prefetch=3,
            grid=(n_blocks, n_slots // bm),
            in_specs=[pl.BlockSpec((bm, d), lambda j, p, te, nt, nx: (jnp.minimum(p, nt[0] - 1), 0)),
                      pl.BlockSpec(memory_space=pl.ANY)],
            out_specs=pl.BlockSpec((bm, bn), lambda j, p, te, nt, nx: (p, j)),
            scratch_shapes=[pltpu.VMEM((d, bn), F32), pltpu.VMEM((d, bn), F32),
                            pltpu.VMEM((d, bn), BF16), pltpu.VMEM((d, bn), BF16),
                            pltpu.SemaphoreType.DMA((2,))]),
        compiler_params=_params(("arbitrary", "arbitrary"), blocks, scratch_bytes=scratch,
                                temp_bytes=3 * _nbytes((bm, bn), F32)),
        name="moe_up",
    )(*tiles, xs, w4)


def _moe_down_body(te_ref, nt_ref, nx_ref, a_ref, w_hbm, o_ref, stage, w_scr, sem, *, layer, bn, n_blocks):
    def copies(e, j):
        return (pltpu.make_async_copy(w_hbm.at[layer, e, :, pl.ds(j * bn, bn)], stage, sem),)

    live = _refresh_expert_weights(te_ref, nt_ref, nx_ref, copies, [stage], [w_scr], n_blocks)

    @pl.when(live)
    def _():
        o_ref[...] = jnp.dot(a_ref[...], w_scr[...], preferred_element_type=F32)

    @pl.when(jnp.logical_not(live))
    def _():
        o_ref[...] = jnp.zeros_like(o_ref)


def _moe_down(act, w4, layer, tiles, bm, bn_pref=1024):
    n_slots, f = act.shape
    d = w4.shape[3]
    bn = _tile(d, bn_pref)
    n_blocks = d // bn
    blocks = _nbytes((bm, f), BF16) + _nbytes((bm, bn), F32)
    scratch = _nbytes((f, bn), F32) + _nbytes((f, bn), BF16)
    return pl.pallas_call(
        functools.partial(_moe_down_body, layer=layer, bn=bn, n_blocks=n_blocks),
        out_shape=jax.ShapeDtypeStruct((n_slots, d), F32),
        grid_spec=pltpu.PrefetchScalarGridSpec(
            num_scalar_prefetch=3,
            grid=(n_blocks, n_slots // bm),
            in_specs=[pl.BlockSpec((bm, f), lambda j, p, te, nt, nx: (jnp.minimum(p, nt[0] - 1), 0)),
                      pl.BlockSpec(memory_space=pl.ANY)],
            out_specs=pl.BlockSpec((bm, bn), lambda j, p, te, nt, nx: (p, j)),
            scratch_shapes=[pltpu.VMEM((f, bn), F32), pltpu.VMEM((f, bn), BF16),
                            pltpu.SemaphoreType.DMA(())]),
        compiler_params=_params(("arbitrary", "arbitrary"), blocks, scratch_bytes=scratch,
                                temp_bytes=_nbytes((bm, bn), F32) + _nbytes((f, bn), BF16)),
        name="moe_down",
    )(*tiles, act, w4)


def _combine_body(pos1_ref, pos2_ref, r_ref, x_ref, ys_hbm, *rest, bt, n_steps, final_norm):
    if final_norm:
        g_ref, o_ref, buf, sems = rest
    else:
        o_ref, buf, sems = rest
    i = pl.program_id(0)
    slot = i % 2

    def issue(step, to_slot):
        def one(r, carry):
            t = step * bt + r
            _row_copy(ys_hbm, pos1_ref[t], buf.at[to_slot, 0], r, sems.at[to_slot]).start()
            _row_copy(ys_hbm, pos2_ref[t], buf.at[to_slot, 1], r, sems.at[to_slot]).start()
            return carry
        lax.fori_loop(0, bt, one, 0, unroll=8)

    @pl.when(i == 0)
    def _():
        issue(0, 0)

    @pl.when(i + 1 < n_steps)
    def _():
        issue(i + 1, 1 - slot)

    def drain(r, carry):
        _row_copy(ys_hbm, 0, buf.at[slot, 0], 0, sems.at[slot]).wait()
        _row_copy(ys_hbm, 0, buf.at[slot, 1], 0, sems.at[slot]).wait()
        return carry

    lax.fori_loop(0, bt, drain, 0, unroll=8)
    w1 = r_ref[:, _R_W1:_R_W1 + 1]
    w2 = r_ref[:, _R_W2:_R_W2 + 1]
    out = x_ref[...] + (w1 * buf[slot, 0] + w2 * buf[slot, 1])
    if final_norm:
        out = (out * lax.rsqrt(jnp.mean(out * out, axis=-1, keepdims=True) + EPS)) * g_ref[...]
    o_ref[...] = out


def _combine(x, ys, route, pos1, pos2, final_gain=None):
    n, d = x.shape
    bt = _tile(n, 256)
    n_steps = n // bt
    final_norm = final_gain is not None
    in_specs = [pl.BlockSpec((bt, LANES), lambda i, p1, p2: (i, 0)),
                pl.BlockSpec((bt, d), lambda i, p1, p2: (i, 0)),
                pl.BlockSpec(memory_space=pl.ANY)]
    args = [pos1, pos2, route, x, ys]
    if final_norm:
        in_specs.append(pl.BlockSpec((1, d), lambda i, p1, p2: (0, 0)))
        args.append(final_gain.astype(F32).reshape(1, d))
    return pl.pallas_call(
        functools.partial(_combine_body, bt=bt, n_steps=n_steps, final_norm=final_norm),
        out_shape=jax.ShapeDtypeStruct((n, d), F32),
        grid_spec=pltpu.PrefetchScalarGridSpec(
            num_scalar_prefetch=2,
            grid=(n_steps,),
            in_specs=in_specs,
            out_specs=pl.BlockSpec((bt, d), lambda i, p1, p2: (i, 0)),
            scratch_shapes=[pltpu.VMEM((2, TOP_K, bt, d), F32), pltpu.SemaphoreType.DMA((2,))]),
        compiler_params=_params(("arbitrary",), 2 * _nbytes((bt, d), F32),
                                scratch_bytes=_nbytes((2, TOP_K, bt, d), F32),
                                temp_bytes=2 * _nbytes((bt, d), F32)),
        name="moe_combine",
    )(*args)


def _moe(x, ln, router_w, router_b, w_gate_up, w_down, layer, final_gain, bm_pref=512):
    n, d = x.shape
    n_exp = router_w.shape[1]
    bm = _tile(n * TOP_K, bm_pref)
    n_tiles_max = (n * TOP_K) // bm + n_exp
    h_f32, logits = _rmsnorm_router(x, ln, router_w, router_b)
    route, counts = _route(logits, n_exp)
    counts = counts[0, :n_exp].astype(jnp.int32)
    sizes = (counts + bm - 1) // bm * bm
    ends = jnp.cumsum(sizes)
    starts = ends - sizes
    e1, e2 = route[:, _R_E1].astype(jnp.int32), route[:, _R_E2].astype(jnp.int32)
    pos1 = starts[e1] + route[:, _R_RANK1].astype(jnp.int32)
    pos2 = starts[e2] + route[:, _R_RANK2].astype(jnp.int32)
    n_tiles = (ends[-1] // bm).astype(jnp.int32).reshape(1)
    tile_start = jnp.arange(n_tiles_max, dtype=jnp.int32) * bm
    tile_expert = jnp.sum(tile_start[:, None] >= ends[None, :], axis=1).astype(jnp.int32)
    tile_expert = jnp.minimum(tile_expert, tile_expert[n_tiles[0] - 1])
    tile = jnp.arange(n_tiles_max, dtype=jnp.int32)
    later_other = ((tile[None, :] > tile[:, None]) & (tile[None, :] < n_tiles[0])
                   & (tile_expert[None, :] != tile_expert[:, None]))
    first_other = jnp.min(jnp.where(later_other, tile[None, :], n_tiles_max), axis=1)
    next_expert = jnp.where(first_other < n_tiles_max,
                            tile_expert[jnp.minimum(first_other, n_tiles_max - 1)], -1).astype(jnp.int32)
    tiles = (tile_expert, n_tiles, next_expert)

    n_slots = n_tiles_max * bm
    xs = _gather_rows(h_f32, _invert_slots(pos1, pos2, n_slots), n_slots)
    act = _moe_up(xs, w_gate_up, layer, tiles, bm)
    ys = _moe_down(act, w_down, layer, tiles, bm)
    return _combine(x, ys, route, pos1, pos2, final_gain)


def kernel(x, ln_mix, ln_ffn, w_in, conv_w, conv_b, i_bias, f_bias, mh_norm, p_conv, p_mlstm, w_out,
           ffn_w_gate_up, ffn_w_down, router_w, router_b, exp_w_gate_up, exp_w_down, final_norm):
    batch, seq, d = x.shape
    depth = ln_mix.shape[0]
    xf = x.reshape(batch * seq, d)
    wt_in = jnp.transpose(w_in, (0, 2, 1))
    normed = False
    for layer in range(depth):
        xf = _hybrid_mixer(xf, ln_mix[layer], wt_in, layer, conv_w[layer], conv_b[layer],
                           i_bias[layer], f_bias[layer], mh_norm[layer], p_conv, p_mlstm, w_out,
                           batch, seq)
        j = layer // 2
        if layer % 2 == 0:
            h = _rmsnorm(xf, ln_ffn[layer], BF16)
            act = _swiglu_up(h, ffn_w_gate_up, j)
            xf = _matmul_residual_ktiled(act, ffn_w_down[j].astype(BF16), xf)
        else:
            normed = layer == depth - 1
            xf = _moe(xf, ln_ffn[layer], router_w[j], router_b[j], exp_w_gate_up, exp_w_down, j,
                      final_norm if normed else None)
    if not normed:
        xf = _rmsnorm(xf, final_norm, x.dtype)
    return xf.reshape(batch, seq, d)
```

```python
import functools

import jax
import jax.numpy as jnp
from jax import lax
from jax.experimental import pallas as pl
from jax.experimental.pallas import tpu as pltpu

F32 = jnp.float32
BF16 = jnp.bfloat16

EPS = 1e-6
GATE_SOFTCAP = 15.0
N_HEADS = 8
TOP_K = 2

V7X_VMEM_BYTES = 64 * 2**20
LANES = 128
BF16_SUBLANES = 16
N_DMA_PRIORITIES = 2
MASKED = -1e30

MLSTM_CHUNK = 256
HEAD_GROUP = 2
N_GATE_LANES = 2 * N_HEADS


def _tile(dim, pref):
    if dim <= pref:
        return dim
    t = pref
    while dim % t:
        t -= LANES
    return t


def _params(semantics, block_bytes, scratch_bytes=0, temp_bytes=0):
    need = 2 * block_bytes + scratch_bytes + temp_bytes + (4 << 20)
    return pltpu.CompilerParams(
        dimension_semantics=semantics,
        vmem_limit_bytes=int(min(max(need, 32 << 20), V7X_VMEM_BYTES - (6 << 20))))


def _nbytes(shape, dtype):
    n = 1
    for s in shape:
        n *= s
    return n * jnp.dtype(dtype).itemsize


def _rmsnorm_body(x_ref, g_ref, o_ref):
    x = x_ref[...]
    inv = lax.rsqrt(jnp.mean(x * x, axis=-1, keepdims=True) + EPS)
    o_ref[...] = ((x * inv) * g_ref[...]).astype(o_ref.dtype)


def _rmsnorm(x, g, out_dtype):
    n, d = x.shape
    bm = _tile(n, 256)
    return pl.pallas_call(
        _rmsnorm_body,
        out_shape=jax.ShapeDtypeStruct((n, d), out_dtype),
        grid=(n // bm,),
        in_specs=[pl.BlockSpec((bm, d), lambda i: (i, 0)),
                  pl.BlockSpec((1, d), lambda i: (0, 0))],
        out_specs=pl.BlockSpec((bm, d), lambda i: (i, 0)),
        compiler_params=_params(("parallel",), _nbytes((bm, d), F32) * 2),
        name="rmsnorm",
    )(x, g.reshape(1, d))


def _rmsnorm_router_body(x_ref, g_ref, whi_ref, wlo_ref, rb_ref, hf_ref, lg_ref):
    x = x_ref[...]
    inv = lax.rsqrt(jnp.mean(x * x, axis=-1, keepdims=True) + EPS)
    h = (x * inv) * g_ref[...]
    h_hi = h.astype(BF16)
    hf_ref[...] = h
    h_lo = (h - h_hi.astype(F32)).astype(BF16)
    lg = (jnp.dot(h_hi, whi_ref[...], preferred_element_type=F32)
          + jnp.dot(h_lo, whi_ref[...], preferred_element_type=F32)
          + jnp.dot(h_hi, wlo_ref[...], preferred_element_type=F32))
    lg_ref[...] = lg + rb_ref[...]


def _rmsnorm_router(x, g, router_w, router_b):
    n, d = x.shape
    n_exp = router_w.shape[1]
    bm = _tile(n, 256)
    w = jnp.pad(router_w.astype(F32), ((0, 0), (0, LANES - n_exp)))
    w_hi = w.astype(BF16)
    w_lo = (w - w_hi.astype(F32)).astype(BF16)
    rb = jnp.pad(router_b.astype(F32), (0, LANES - n_exp)).reshape(1, LANES)
    row = lambda i: (i, 0)
    fixed = lambda i: (0, 0)
    return pl.pallas_call(
        _rmsnorm_router_body,
        out_shape=(jax.ShapeDtypeStruct((n, d), F32), jax.ShapeDtypeStruct((n, LANES), F32)),
        grid=(n // bm,),
        in_specs=[pl.BlockSpec((bm, d), row), pl.BlockSpec((1, d), fixed),
                  pl.BlockSpec((d, LANES), fixed), pl.BlockSpec((d, LANES), fixed),
                  pl.BlockSpec((1, LANES), fixed)],
        out_specs=(pl.BlockSpec((bm, d), row), pl.BlockSpec((bm, LANES), row)),
        compiler_params=_params(("parallel",), _nbytes((bm, d), F32) * 2 + _nbytes((d, LANES), BF16) * 2),
        name="rmsnorm_router",
    )(x, g.reshape(1, d), w_hi, w_lo, rb)


def _refresh_weights(copies, stages, w_scrs, n_blocks):
    j = pl.program_id(0)

    @pl.when(pl.program_id(1) == 0)
    def _():
        @pl.when(j == 0)
        def _():
            for copy in copies:
                copy(0).start()

        for copy, stage, w_scr in zip(copies, stages, w_scrs):
            copy(j).wait()
            w_scr[...] = stage[...].astype(BF16)

        @pl.when(j + 1 < n_blocks)
        def _():
            for copy in copies:
                copy(j + 1).start()


def _in_proj_body(x_ref, wt_hbm, o_ref, stage, w_scr, sem, *, layer, bn, n_blocks, n_main_blocks):
    def copy(j):
        row0 = j * bn + jnp.where(j >= n_main_blocks, N_GATE_LANES, 0)
        return pltpu.make_async_copy(wt_hbm.at[layer, pl.ds(row0, bn)], stage, sem)

    _refresh_weights([copy], [stage], [w_scr], n_blocks)
    o_ref[...] = lax.dot_general(x_ref[...], w_scr[...], (((1,), (1,)), ((), ())),
                                 preferred_element_type=F32).astype(o_ref.dtype)


def _in_proj(x, wt, layer, d, bm_pref=1024, bn_pref=1024):
    m, k = x.shape
    dc = d // 2
    bm, bn = _tile(m, bm_pref), _tile(dc, bn_pref)
    nb = dc // bn
    n_units = _N_MAIN_UNITS + 4
    n_blocks = n_units * nb
    out_block = lambda j: jnp.where(j < 3 * nb, j + 10 * nb, jnp.where(j < 5 * nb, j + 5 * nb, j - 5 * nb))
    blocks = _nbytes((bm, k), x.dtype) + _nbytes((bm, bn), BF16)
    scratch = _nbytes((bn, k), F32) + _nbytes((bn, k), BF16)
    return pl.pallas_call(
        functools.partial(_in_proj_body, layer=layer, bn=bn, n_blocks=n_blocks,
                          n_main_blocks=_N_MAIN_UNITS * nb),
        out_shape=jax.ShapeDtypeStruct((m, n_units * dc), BF16),
        grid=(n_blocks, m // bm),
        in_specs=[pl.BlockSpec((bm, k), lambda j, i: (i, 0)),
                  pl.BlockSpec(memory_space=pl.ANY)],
        out_specs=pl.BlockSpec((bm, bn), lambda j, i: (i, out_block(j))),
        scratch_shapes=[pltpu.VMEM((bn, k), F32), pltpu.VMEM((bn, k), BF16), pltpu.SemaphoreType.DMA(())],
        compiler_params=_params(("arbitrary", "arbitrary"), blocks, scratch_bytes=scratch,
                                temp_bytes=_nbytes((bm, bn), F32)),
        name="in_proj",
    )(x, wt)


def _matmul_residual_body(a_ref, w_hbm, x_ref, o_ref, stage, w_scr, sem, *, layer, bn, n_blocks):
    def copy(j):
        return pltpu.make_async_copy(w_hbm.at[layer, :, pl.ds(j * bn, bn)], stage, sem)

    _refresh_weights([copy], [stage], [w_scr], n_blocks)
    o_ref[...] = x_ref[...] + jnp.dot(a_ref[...], w_scr[...], preferred_element_type=F32)


def _matmul_residual(a, w3, layer, x, bm_pref=1024, bn_pref=512):
    m, k = a.shape
    n = w3.shape[2]
    bm, bn = _tile(m, bm_pref), _tile(n, bn_pref)
    n_blocks = n // bn
    blocks = _nbytes((bm, k), a.dtype) + 2 * _nbytes((bm, bn), F32)
    scratch = _nbytes((k, bn), F32) + _nbytes((k, bn), BF16)
    return pl.pallas_call(
        functools.partial(_matmul_residual_body, layer=layer, bn=bn, n_blocks=n_blocks),
        out_shape=jax.ShapeDtypeStruct((m, n), F32),
        grid=(n_blocks, m // bm),
        in_specs=[pl.BlockSpec((bm, k), lambda j, i: (i, 0)),
                  pl.BlockSpec(memory_space=pl.ANY),
                  pl.BlockSpec((bm, bn), lambda j, i: (i, j))],
        out_specs=pl.BlockSpec((bm, bn), lambda j, i: (i, j)),
        scratch_shapes=[pltpu.VMEM((k, bn), F32), pltpu.VMEM((k, bn), BF16), pltpu.SemaphoreType.DMA(())],
        compiler_params=_params(("arbitrary", "arbitrary"), blocks, scratch_bytes=scratch,
                                temp_bytes=_nbytes((bm, bn), F32)),
        name="out_proj",
    )(a, w3, x)


def _matmul_residual_ktiled_body(a_ref, w_ref, x_ref, o_ref):
    @pl.when(pl.program_id(2) == 0)
    def _():
        o_ref[...] = x_ref[...]

    o_ref[...] += jnp.dot(a_ref[...], w_ref[...], preferred_element_type=F32)


def _matmul_residual_ktiled(a, w, x, bm_pref=1024, bn_pref=1024, bk_pref=3584):
    m, k = a.shape
    n = w.shape[1]
    bm, bn, bk = _tile(m, bm_pref), _tile(n, bn_pref), _tile(k, bk_pref)
    blocks = _nbytes((bm, bk), a.dtype) + _nbytes((bk, bn), w.dtype) + 2 * _nbytes((bm, bn), F32)
    return pl.pallas_call(
        _matmul_residual_ktiled_body,
        out_shape=jax.ShapeDtypeStruct((m, n), F32),
        grid=(m // bm, n // bn, k // bk),
        in_specs=[pl.BlockSpec((bm, bk), lambda i, j, kk: (i, kk)),
                  pl.BlockSpec((bk, bn), lambda i, j, kk: (kk, j)),
                  pl.BlockSpec((bm, bn), lambda i, j, kk: (i, j))],
        out_specs=pl.BlockSpec((bm, bn), lambda i, j, kk: (i, j)),
        compiler_params=_params(("parallel", "parallel", "arbitrary"), blocks,
                                temp_bytes=_nbytes((bm, bn), F32)),
        name="ffn_down",
    )(a, w, x)


def _swiglu(h, wg, wu):
    g = jnp.dot(h, wg, preferred_element_type=F32)
    u = jnp.dot(h, wu, preferred_element_type=F32)
    return g * jax.nn.sigmoid(g) * u


def _swiglu_up_body(h_ref, w_hbm, o_ref, g_stage, u_stage, wg_scr, wu_scr, sems, *, layer, bn, n_blocks):
    def copy_gate(j):
        return pltpu.make_async_copy(w_hbm.at[layer, :, pl.ds(j * bn, bn)], g_stage, sems.at[0])

    def copy_up(j):
        return pltpu.make_async_copy(w_hbm.at[layer, :, pl.ds((n_blocks + j) * bn, bn)], u_stage, sems.at[1])

    _refresh_weights([copy_gate, copy_up], [g_stage, u_stage], [wg_scr, wu_scr], n_blocks)
    o_ref[...] = _swiglu(h_ref[...], wg_scr[...], wu_scr[...]).astype(o_ref.dtype)


def _swiglu_up(h, w3, layer, bm_pref=1024, bn_pref=512):
    m, k = h.shape
    f = w3.shape[2] // 2
    bm, bn = _tile(m, bm_pref), _tile(f, bn_pref)
    n_blocks = f // bn
    blocks = _nbytes((bm, k), h.dtype) + _nbytes((bm, bn), BF16)
    scratch = 2 * (_nbytes((k, bn), F32) + _nbytes((k, bn), BF16))
    return pl.pallas_call(
        functools.partial(_swiglu_up_body, layer=layer, bn=bn, n_blocks=n_blocks),
        out_shape=jax.ShapeDtypeStruct((m, f), BF16),
        grid=(n_blocks, m // bm),
        in_specs=[pl.BlockSpec((bm, k), lambda j, i: (i, 0)),
                  pl.BlockSpec(memory_space=pl.ANY)],
        out_specs=pl.BlockSpec((bm, bn), lambda j, i: (i, j)),
        scratch_shapes=[pltpu.VMEM((k, bn), F32), pltpu.VMEM((k, bn), F32),
                        pltpu.VMEM((k, bn), BF16), pltpu.VMEM((k, bn), BF16),
                        pltpu.SemaphoreType.DMA((2,))],
        compiler_params=_params(("arbitrary", "arbitrary"), blocks, scratch_bytes=scratch,
                                temp_bytes=3 * _nbytes((bm, bn), F32)),
        name="ffn_up",
    )(h, w3)


_COL_V, _COL_O, _COL_GA, _COL_GB = 0, 1, 2, 3
_COL_Q, _COL_K, _COL_BG, _COL_CG, _COL_VC = 8, 9, 10, 11, 12
_N_MAIN_UNITS = 9


def _gates_body(x_ref, wt_hbm, b_ref, gc_ref, gr_ref, stage, w_scr, sem, *, layer, row0):
    @pl.when(pl.program_id(0) == 0)
    def _():
        stage[...] = jnp.zeros_like(stage)
        rows = pltpu.make_async_copy(wt_hbm.at[layer, pl.ds(row0, N_GATE_LANES)],
                                     stage.at[pl.ds(0, N_GATE_LANES)], sem)
        rows.start()
        rows.wait()
        w_scr[...] = stage[...].astype(BF16)

    z = lax.dot_general(x_ref[...], w_scr[...], (((1,), (1,)), ((), ())),
                        preferred_element_type=F32) + b_ref[...]
    sc = GATE_SOFTCAP * jnp.tanh(z / GATE_SOFTCAP)
    logf = jnp.minimum(sc, 0.0) - jnp.log1p(jnp.exp(-jnp.abs(sc)))
    rows = z.shape[0]
    row = lax.broadcasted_iota(jnp.int32, z.shape, 0)
    cum = logf
    shift = 1
    while shift < rows:
        cum = cum + jnp.where(row >= shift, pltpu.roll(cum, shift, 0), 0.0)
        shift *= 2
    lane = lax.broadcasted_iota(jnp.int32, z.shape, 1)
    gc = jnp.where(lane < N_HEADS, sc, cum)
    gc_ref[...] = gc
    gr_ref[0] = gc.T[:N_GATE_LANES, :]


def _gates(xn, wt, layer, i_bias, f_bias, chunk):
    n, d = xn.shape
    bias = jnp.pad(jnp.concatenate([i_bias, f_bias]).astype(F32), (0, LANES - N_GATE_LANES))
    return pl.pallas_call(
        functools.partial(_gates_body, layer=layer, row0=_N_MAIN_UNITS * (d // 2)),
        out_shape=(jax.ShapeDtypeStruct((n, LANES), F32),
                   jax.ShapeDtypeStruct((n // chunk, N_GATE_LANES, chunk), F32)),
        grid=(n // chunk,),
        in_specs=[pl.BlockSpec((chunk, d), lambda i: (i, 0)),
                  pl.BlockSpec(memory_space=pl.ANY),
                  pl.BlockSpec((1, LANES), lambda i: (0, 0))],
        out_specs=(pl.BlockSpec((chunk, LANES), lambda i: (i, 0)),
                   pl.BlockSpec((1, N_GATE_LANES, chunk), lambda i: (i, 0, 0))),
        scratch_shapes=[pltpu.VMEM((LANES, d), F32), pltpu.VMEM((LANES, d), BF16),
                        pltpu.SemaphoreType.DMA(())],
        compiler_params=_params(("arbitrary",), _nbytes((chunk, d), BF16),
                                scratch_bytes=_nbytes((LANES, d), F32) + _nbytes((LANES, d), BF16)),
        name="mlstm_gates",
    )(xn, wt, bias.reshape(1, LANES))


def _wide(col, n):
    return col if n == LANES else jnp.concatenate([col] * (n // LANES), axis=1)


def _mlstm_body(q_ref, k_ref, v_ref, o_ref, gc_ref, gr_ref, g_ref, y_ref, c_scr, n_scr, m_scr,
                *, dqk, dv):
    @pl.when(pl.program_id(1) == 0)
    def _():
        c_scr[...] = jnp.zeros_like(c_scr)
        n_scr[...] = jnp.zeros_like(n_scr)
        m_scr[...] = jnp.zeros_like(m_scr)

    chunk = q_ref.shape[0]
    t_idx = lax.broadcasted_iota(jnp.int32, (chunk, chunk), 0)
    s_idx = lax.broadcasted_iota(jnp.int32, (chunk, chunk), 1)
    causal = s_idx <= t_idx
    gc = gc_ref[...]
    gr = gr_ref[0]
    scale = dqk ** -0.5
    for h0 in range(0, N_HEADS, HEAD_GROUP):
        heads = range(h0, h0 + HEAD_GROUP)
        st = {h: {} for h in heads}
        for h in heads:
            t = st[h]
            t["q"] = q_ref[:, h * dqk:(h + 1) * dqk]
            t["kf"] = k_ref[:, h * dqk:(h + 1) * dqk].astype(F32) * scale
            t["v"] = v_ref[:, h * dv:(h + 1) * dv]
            t["i_col"] = jnp.broadcast_to(gc[:, h:h + 1], (chunk, LANES))
            t["b_col"] = jnp.broadcast_to(gc[:, N_HEADS + h:N_HEADS + h + 1], (chunk, LANES))
            i_row, b_row = gr[h:h + 1, :], gr[N_HEADS + h:N_HEADS + h + 1, :]
            t["m_prev"] = m_scr[h]
            t["d_log"] = jnp.where(causal, _wide(t["b_col"], chunk) - b_row + i_row, MASKED)
            t["inter"] = t["b_col"] + t["m_prev"]
        for h in heads:
            t = st[h]
            t["m_t"] = jnp.maximum(t["inter"], jnp.max(t["d_log"], axis=-1, keepdims=True))
            t["w_inter"] = jnp.exp(t["inter"] - t["m_t"])
        for h in heads:
            t = st[h]
            t["s_qk"] = (lax.dot_general(t["q"], t["kf"].astype(BF16), (((1,), (1,)), ((), ())),
                                         preferred_element_type=F32)
                         * jnp.exp(t["d_log"] - _wide(t["m_t"], chunk)))
        for h in heads:
            t = st[h]
            t["c_mat"] = c_scr[h]
            t["n_vec"] = n_scr[h]
            t["num"] = (_wide(t["w_inter"], dv)
                        * jnp.dot(t["q"], t["c_mat"].astype(BF16), preferred_element_type=F32)
                        + jnp.dot(t["s_qk"].astype(BF16), t["v"], preferred_element_type=F32))
            t["den"] = (t["w_inter"] * jnp.sum(t["q"].astype(F32) * t["n_vec"], axis=-1, keepdims=True)
                        + jnp.sum(t["s_qk"], axis=-1, keepdims=True))
        for h in heads:
            t = st[h]
            m_new = t["m_t"][chunk - 1:chunk, :1]
            b_last = t["b_col"][chunk - 1:chunk, :1]
            decay = jnp.exp(b_last + t["m_prev"] - m_new)
            kw = t["kf"] * _wide(jnp.exp(b_last - t["b_col"] + t["i_col"] - m_new), dqk)
            c_scr[h] = decay * t["c_mat"] + lax.dot_general(kw.astype(BF16), t["v"], (((0,), (0,)), ((), ())),
                                                            preferred_element_type=F32)
            n_scr[h] = decay * t["n_vec"] + jnp.sum(kw, axis=0, keepdims=True)
            m_scr[h] = m_new
        for h in heads:
            t = st[h]
            hid = t["num"] * _wide(1.0 / jnp.maximum(jnp.abs(t["den"]), jnp.exp(-t["m_t"])), dv)
            hid = hid * lax.rsqrt(jnp.mean(hid * hid, axis=-1, keepdims=True) + EPS)
            hid = hid * g_ref[:, h * dv:(h + 1) * dv]
            gate = jax.nn.sigmoid(o_ref[:, h * dv:(h + 1) * dv].astype(F32))
            y_ref[:, h * dv:(h + 1) * dv] = (gate * hid).astype(y_ref.dtype)


def _mlstm(y, gc, gr, mh_norm, batch, seq, d, chunk):
    n = batch * seq
    nc = seq // chunk
    dqk, dv = d // 2 // N_HEADS, d // N_HEADS
    rows = lambda col: (lambda b, c: (b * nc + c, col))
    blocks = (2 * _nbytes((chunk, d // 2), BF16) + 3 * _nbytes((chunk, d), BF16)
              + _nbytes((chunk, LANES), F32))
    state = _nbytes((N_HEADS, dqk, dv), F32)
    return pl.pallas_call(
        functools.partial(_mlstm_body, dqk=dqk, dv=dv),
        out_shape=jax.ShapeDtypeStruct((n, d), BF16),
        grid=(batch, nc),
        in_specs=[pl.BlockSpec((chunk, d // 2), rows(_COL_Q)),
                  pl.BlockSpec((chunk, d // 2), rows(_COL_K)),
                  pl.BlockSpec((chunk, d), rows(_COL_V)),
                  pl.BlockSpec((chunk, d), rows(_COL_O)),
                  pl.BlockSpec((chunk, LANES), rows(0)),
                  pl.BlockSpec((1, N_GATE_LANES, chunk), lambda b, c: (b * nc + c, 0, 0)),
                  pl.BlockSpec((1, d), lambda b, c: (0, 0))],
        out_specs=pl.BlockSpec((chunk, d), rows(0)),
        scratch_shapes=[pltpu.VMEM((N_HEADS, dqk, dv), F32),
                        pltpu.VMEM((N_HEADS, 1, dqk), F32),
                        pltpu.VMEM((N_HEADS, 1, 1), F32)],
        compiler_params=_params(("parallel", "arbitrary"), blocks, scratch_bytes=state,
                                temp_bytes=8 << 20),
        name="mlstm",
    )(y, y, y, y, gc, gr, mh_norm.astype(F32).reshape(1, d))


def _conv_body(bg_ref, cg_ref, vc_ref, cgp_ref, vcp_ref, w_ref, b_ref, o_ref, *, tiles_per_seq):
    u = cg_ref[...].astype(F32) * vc_ref[...].astype(F32)
    keep = (pl.program_id(0) % tiles_per_seq != 0).astype(F32)
    halo = cgp_ref[...].astype(F32) * vcp_ref[...].astype(F32) * keep
    prev1 = halo[BF16_SUBLANES - 1:BF16_SUBLANES, :]
    prev2 = halo[BF16_SUBLANES - 2:BF16_SUBLANES - 1, :]
    row = lax.broadcasted_iota(jnp.int32, u.shape, 0)
    u1 = jnp.where(row == 0, prev1, pltpu.roll(u, 1, 0))
    u2 = jnp.where(row == 0, prev2, jnp.where(row == 1, prev1, pltpu.roll(u, 2, 0)))
    w = w_ref[...]
    conv = w[0:1, :] * u2 + w[1:2, :] * u1 + w[2:3, :] * u + b_ref[...]
    o_ref[...] = (bg_ref[...].astype(F32) * conv).astype(o_ref.dtype)


def _gated_conv(y, conv_w, conv_b, seq, d):
    n = y.shape[0]
    dc = d // 2
    bm, bc = _tile(seq, 512), _tile(dc, 1024)
    ncol = dc // bc
    halo_rows = bm // BF16_SUBLANES
    cur = lambda col: (lambda i, j: (i, col * ncol + j))
    prev = lambda col: (lambda i, j: (jnp.maximum(i * halo_rows - 1, 0), col * ncol + j))
    return pl.pallas_call(
        functools.partial(_conv_body, tiles_per_seq=seq // bm),
        out_shape=jax.ShapeDtypeStruct((n, dc), BF16),
        grid=(n // bm, ncol),
        in_specs=[pl.BlockSpec((bm, bc), cur(_COL_BG)),
                  pl.BlockSpec((bm, bc), cur(_COL_CG)),
                  pl.BlockSpec((bm, bc), cur(_COL_VC)),
                  pl.BlockSpec((BF16_SUBLANES, bc), prev(_COL_CG)),
                  pl.BlockSpec((BF16_SUBLANES, bc), prev(_COL_VC)),
                  pl.BlockSpec((conv_w.shape[0], bc), lambda i, j: (0, j)),
                  pl.BlockSpec((1, bc), lambda i, j: (0, j))],
        out_specs=pl.BlockSpec((bm, bc), lambda i, j: (i, j)),
        compiler_params=_params(("parallel", "parallel"), 4 * _nbytes((bm, bc), BF16),
                                temp_bytes=8 * _nbytes((bm, bc), F32)),
        name="gated_conv",
    )(y, y, y, y, y, conv_w.astype(F32), conv_b.astype(F32).reshape(1, dc))


def _merge_body(ya_ref, yb_ref, ga_ref, gb_ref, pc_hbm, pm_hbm, o_ref, pc_stage, pm_stage, pc_scr, pm_scr,
                sems, *, layer, bn, n_blocks):
    def copy_conv(j):
        return pltpu.make_async_copy(pc_hbm.at[layer, :, pl.ds(j * bn, bn)], pc_stage, sems.at[0])

    def copy_mlstm(j):
        return pltpu.make_async_copy(pm_hbm.at[layer, :, pl.ds(j * bn, bn)], pm_stage, sems.at[1])

    _refresh_weights([copy_conv, copy_mlstm], [pc_stage, pm_stage], [pc_scr, pm_scr], n_blocks)
    a = jnp.dot(ya_ref[...], pc_scr[...], preferred_element_type=F32)
    b = jnp.dot(yb_ref[...], pm_scr[...], preferred_element_type=F32)
    merged = (jax.nn.sigmoid(ga_ref[...].astype(F32)) * a
              + jax.nn.sigmoid(gb_ref[...].astype(F32)) * b)
    o_ref[...] = merged.astype(o_ref.dtype)


def _merge(y_a, y_b, p_conv3, p_mlstm3, layer, y, d, bm_pref=1024, bn_pref=512):
    n = y_a.shape[0]
    bm, bn = _tile(n, bm_pref), _tile(d, bn_pref)
    n_blocks = d // bn
    blocks = _nbytes((bm, d // 2), BF16) + _nbytes((bm, d), BF16) + 3 * _nbytes((bm, bn), BF16)
    scratch = (_nbytes((d // 2, bn), F32) + _nbytes((d, bn), F32)
               + _nbytes((d // 2, bn), BF16) + _nbytes((d, bn), BF16))
    return pl.pallas_call(
        functools.partial(_merge_body, layer=layer, bn=bn, n_blocks=n_blocks),
        out_shape=jax.ShapeDtypeStruct((n, d), BF16),
        grid=(n_blocks, n // bm),
        in_specs=[pl.BlockSpec((bm, d // 2), lambda j, i: (i, 0)),
                  pl.BlockSpec((bm, d), lambda j, i: (i, 0)),
                  pl.BlockSpec((bm, bn), lambda j, i: (i, _COL_GA * n_blocks + j)),
                  pl.BlockSpec((bm, bn), lambda j, i: (i, _COL_GB * n_blocks + j)),
                  pl.BlockSpec(memory_space=pl.ANY),
                  pl.BlockSpec(memory_space=pl.ANY)],
        out_specs=pl.BlockSpec((bm, bn), lambda j, i: (i, j)),
        scratch_shapes=[pltpu.VMEM((d // 2, bn), F32), pltpu.VMEM((d, bn), F32),
                        pltpu.VMEM((d // 2, bn), BF16), pltpu.VMEM((d, bn), BF16),
                        pltpu.SemaphoreType.DMA((2,))],
        compiler_params=_params(("arbitrary", "arbitrary"), blocks, scratch_bytes=scratch,
                                temp_bytes=3 * _nbytes((bm, bn), F32)),
        name="merge",
    )(y_a, y_b, y, y, p_conv3, p_mlstm3)


def _hybrid_mixer(x, ln, wt_in, layer, conv_w, conv_b, i_bias, f_bias, mh_norm, p_conv, p_mlstm, w_out,
                  batch, seq):
    n, d = x.shape
    chunk = _tile(seq, MLSTM_CHUNK)
    xn = _rmsnorm(x, ln, BF16)
    y = _in_proj(xn, wt_in, layer, d)
    gc, gr = _gates(xn, wt_in, layer, i_bias, f_bias, chunk)
    y_b = _mlstm(y, gc, gr, mh_norm, batch, seq, d, chunk)
    y_a = _gated_conv(y, conv_w, conv_b, seq, d)
    merged = _merge(y_a, y_b, p_conv, p_mlstm, layer, y, d)
    return _matmul_residual(merged, w_out, layer, x)


_R_E1, _R_E2, _R_RANK1, _R_RANK2, _R_W1, _R_W2 = 0, 1, 2, 3, 4, 5


def _route_body(lg_ref, r_ref, cnt_ref, carry, *, n_exp):
    @pl.when(pl.program_id(0) == 0)
    def _():
        carry[...] = jnp.zeros_like(carry)

    lg = lg_ref[...]
    bt = lg.shape[0]
    lane_i = lax.broadcasted_iota(jnp.int32, lg.shape, 1)
    lane = lane_i.astype(F32)
    x1 = jnp.where(lane_i < n_exp, lg, MASKED)
    m1 = jnp.max(x1, axis=-1, keepdims=True)
    e1 = jnp.min(jnp.where(x1 == m1, lane, float(LANES)), axis=-1, keepdims=True)
    x2 = jnp.where(lane == e1, MASKED, x1)
    m2 = jnp.max(x2, axis=-1, keepdims=True)
    e2 = jnp.min(jnp.where(x2 == m2, lane, float(LANES)), axis=-1, keepdims=True)
    ex = jnp.exp(m2 - m1)
    w1 = 1.0 / (1.0 + ex)
    w2 = ex / (1.0 + ex)
    hit1, hit2 = lane == e1, lane == e2
    onehot = (hit1 | hit2).astype(F32)
    t_idx = lax.broadcasted_iota(jnp.int32, (bt, bt), 0)
    s_idx = lax.broadcasted_iota(jnp.int32, (bt, bt), 1)
    earlier = (s_idx < t_idx).astype(BF16)
    rank = jnp.dot(earlier, onehot.astype(BF16), preferred_element_type=F32) + carry[...]
    rank1 = jnp.sum(jnp.where(hit1, rank, 0.0), axis=-1, keepdims=True)
    rank2 = jnp.sum(jnp.where(hit2, rank, 0.0), axis=-1, keepdims=True)
    carry[...] += jnp.sum(onehot, axis=0, keepdims=True)
    rec = jnp.zeros_like(lg)
    for idx, val in ((_R_E1, e1), (_R_E2, e2), (_R_RANK1, rank1), (_R_RANK2, rank2),
                     (_R_W1, w1), (_R_W2, w2)):
        rec = jnp.where(lane_i == idx, val, rec)
    r_ref[...] = rec
    cnt_ref[...] = jnp.broadcast_to(carry[...], cnt_ref.shape)


def _route(logits, n_exp):
    n = logits.shape[0]
    bt = _tile(n, 512)
    return pl.pallas_call(
        functools.partial(_route_body, n_exp=n_exp),
        out_shape=(jax.ShapeDtypeStruct((n, LANES), F32), jax.ShapeDtypeStruct((8, LANES), F32)),
        grid=(n // bt,),
        in_specs=[pl.BlockSpec((bt, LANES), lambda i: (i, 0))],
        out_specs=(pl.BlockSpec((bt, LANES), lambda i: (i, 0)),
                   pl.BlockSpec((8, LANES), lambda i: (0, 0))),
        scratch_shapes=[pltpu.VMEM((1, LANES), F32)],
        compiler_params=_params(("arbitrary",), 2 * _nbytes((bt, LANES), F32),
                                temp_bytes=_nbytes((bt, bt), F32)),
        name="moe_route",
    )(logits)


def _row_copy(src_hbm, src_row, dst_ref, dst_row, sem):
    return pltpu.make_async_copy(src_hbm.at[pl.ds(src_row, 1)], dst_ref.at[pl.ds(dst_row, 1)], sem)


def _invert_slots_body(pos1_ref, pos2_ref, tok_ref, *, n, n_slots):
    def clear(s, carry):
        tok_ref[s] = 0
        return carry

    def put(t, carry):
        tok_ref[pos1_ref[t]] = t
        tok_ref[pos2_ref[t]] = t
        return carry

    lax.fori_loop(0, n_slots, clear, 0, unroll=8)
    lax.fori_loop(0, n, put, 0, unroll=8)


def _invert_slots(pos1, pos2, n_slots):
    smem = pl.BlockSpec(memory_space=pltpu.SMEM)
    return pl.pallas_call(
        functools.partial(_invert_slots_body, n=pos1.shape[0], n_slots=n_slots),
        out_shape=jax.ShapeDtypeStruct((n_slots,), jnp.int32),
        in_specs=[smem, smem],
        out_specs=smem,
        name="moe_invert",
    )(pos1, pos2)


def _gather_rows_body(tok_ref, h_hbm, o_ref, buf, sems, *, bt, n_steps):
    i = pl.program_id(0)
    slot = i % 2

    def issue(step, to_slot):
        def pair(half, carry):
            for lane in range(N_DMA_PRIORITIES):
                r = N_DMA_PRIORITIES * half + lane
                _row_copy(h_hbm, tok_ref[step * bt + r], buf.at[to_slot], r,
                          sems.at[to_slot]).start(priority=lane)
            return carry
        lax.fori_loop(0, bt // N_DMA_PRIORITIES, pair, 0, unroll=4)

    @pl.when(i == 0)
    def _():
        issue(0, 0)

    @pl.when(i + 1 < n_steps)
    def _():
        issue(i + 1, 1 - slot)

    def drain(r, carry):
        _row_copy(h_hbm, 0, buf.at[slot], 0, sems.at[slot]).wait()
        return carry

    lax.fori_loop(0, bt, drain, 0, unroll=8)
    o_ref[...] = buf[slot].astype(o_ref.dtype)


def _gather_rows(h, tok, n_slots):
    d = h.shape[1]
    bt = _tile(n_slots, 256)
    n_steps = n_slots // bt
    return pl.pallas_call(
        functools.partial(_gather_rows_body, bt=bt, n_steps=n_steps),
        out_shape=jax.ShapeDtypeStruct((n_slots, d), BF16),
        grid_spec=pltpu.PrefetchScalarGridSpec(
            num_scalar_prefetch=1,
            grid=(n_steps,),
            in_specs=[pl.BlockSpec(memory_space=pl.ANY)],
            out_specs=pl.BlockSpec((bt, d), lambda i, tok: (i, 0)),
            scratch_shapes=[pltpu.VMEM((2, bt, d), F32), pltpu.SemaphoreType.DMA((2,))]),
        compiler_params=_params(("arbitrary",), _nbytes((bt, d), BF16),
                                scratch_bytes=_nbytes((2, bt, d), F32), temp_bytes=_nbytes((bt, d), F32)),
        name="moe_gather",
    )(tok, h)


def _refresh_expert_weights(te_ref, nt_ref, nx_ref, copies, stages, w_scrs, n_blocks):
    j, p = pl.program_id(0), pl.program_id(1)
    live = p < nt_ref[0]
    fresh = live & ((p == 0) | (te_ref[p] != te_ref[jnp.maximum(p - 1, 0)]))

    @pl.when(fresh)
    def _():
        @pl.when((j == 0) & (p == 0))
        def _():
            for copy in copies(te_ref[0], 0):
                copy.start()

        for copy, stage, w_scr in zip(copies(te_ref[p], j), stages, w_scrs):
            copy.wait()
            w_scr[...] = stage[...].astype(BF16)

        more = nx_ref[p] >= 0

        @pl.when(more)
        def _():
            for copy in copies(nx_ref[p], j):
                copy.start()

        @pl.when(jnp.logical_not(more) & (j + 1 < n_blocks))
        def _():
            for copy in copies(te_ref[0], j + 1):
                copy.start()

    return live


def _moe_up_body(te_ref, nt_ref, nx_ref, x_ref, w_hbm, o_ref, g_stage, u_stage, wg_scr, wu_scr, sems,
                 *, layer, bn, n_blocks):
    def copies(e, j):
        return (pltpu.make_async_copy(w_hbm.at[layer, e, :, pl.ds(j * bn, bn)], g_stage, sems.at[0]),
                pltpu.make_async_copy(w_hbm.at[layer, e, :, pl.ds((n_blocks + j) * bn, bn)], u_stage,
                                      sems.at[1]))

    live = _refresh_expert_weights(te_ref, nt_ref, nx_ref, copies, [g_stage, u_stage], [wg_scr, wu_scr],
                                   n_blocks)

    @pl.when(live)
    def _():
        o_ref[...] = _swiglu(x_ref[...], wg_scr[...], wu_scr[...]).astype(o_ref.dtype)

    @pl.when(jnp.logical_not(live))
    def _():
        o_ref[...] = jnp.zeros_like(o_ref)


def _moe_up(xs, w4, layer, tiles, bm, bn_pref=512):
    n_slots, d = xs.shape
    f = w4.shape[3] // 2
    bn = _tile(f, bn_pref)
    n_blocks = f // bn
    blocks = _nbytes((bm, d), xs.dtype) + _nbytes((bm, bn), BF16)
    scratch = 2 * (_nbytes((d, bn), F32) + _nbytes((d, bn), BF16))
    return pl.pallas_call(
        functools.partial(_moe_up_body, layer=layer, bn=bn, n_blocks=n_blocks),
        out_shape=jax.ShapeDtypeStruct((n_slots, f), BF16),
        grid_spec=pltpu.PrefetchScalarGridSpec(
            num_scalar_prefetch=3,
            grid=(n_blocks, n_slots // bm),
            in_specs=[pl.BlockSpec((bm, d), lambda j, p, te, nt, nx: (jnp.minimum(p, nt[0] - 1), 0)),
                      pl.BlockSpec(memory_space=pl.ANY)],
            out_specs=pl.BlockSpec((bm, bn), lambda j, p, te, nt, nx: (p, j)),
            scratch_shapes=[pltpu.VMEM((d, bn), F32), pltpu.VMEM((d, bn), F32),
                            pltpu.VMEM((d, bn), BF16), pltpu.VMEM((d, bn), BF16),
                            pltpu.SemaphoreType.DMA((2,))]),
        compiler_params=_params(("arbitrary", "arbitrary"), blocks, scratch_bytes=scratch,
                                temp_bytes=3 * _nbytes((bm, bn), F32)),
        name="moe_up",
    )(*tiles, xs, w4)


def _moe_down_body(te_ref, nt_ref, nx_ref, a_ref, w_hbm, o_ref, stage, w_scr, sem, *, layer, bn, n_blocks):
    def copies(e, j):
        return (pltpu.make_async_copy(w_hbm.at[layer, e, :, pl.ds(j * bn, bn)], stage, sem),)

    live = _refresh_expert_weights(te_ref, nt_ref, nx_ref, copies, [stage], [w_scr], n_blocks)

    @pl.when(live)
    def _():
        o_ref[...] = jnp.dot(a_ref[...], w_scr[...], preferred_element_type=F32)

    @pl.when(jnp.logical_not(live))
    def _():
        o_ref[...] = jnp.zeros_like(o_ref)


def _moe_down(act, w4, layer, tiles, bm, bn_pref=1024):
    n_slots, f = act.shape
    d = w4.shape[3]
    bn = _tile(d, bn_pref)
    n_blocks = d // bn
    blocks = _nbytes((bm, f), BF16) + _nbytes((bm, bn), F32)
    scratch = _nbytes((f, bn), F32) + _nbytes((f, bn), BF16)
    return pl.pallas_call(
        functools.partial(_moe_down_body, layer=layer, bn=bn, n_blocks=n_blocks),
        out_shape=jax.ShapeDtypeStruct((n_slots, d), F32),
        grid_spec=pltpu.PrefetchScalarGridSpec(
            num_scalar_prefetch=3,
            grid=(n_blocks, n_slots // bm),
            in_specs=[pl.BlockSpec((bm, f), lambda j, p, te, nt, nx: (jnp.minimum(p, nt[0] - 1), 0)),
                      pl.BlockSpec(memory_space=pl.ANY)],
            out_specs=pl.BlockSpec((bm, bn), lambda j, p, te, nt, nx: (p, j)),
            scratch_shapes=[pltpu.VMEM((f, bn), F32), pltpu.VMEM((f, bn), BF16),
                            pltpu.SemaphoreType.DMA(())]),
        compiler_params=_params(("arbitrary", "arbitrary"), blocks, scratch_bytes=scratch,
                                temp_bytes=_nbytes((bm, bn), F32) + _nbytes((f, bn), BF16)),
        name="moe_down",
    )(*tiles, act, w4)


def _combine_body(pos1_ref, pos2_ref, r_ref, x_ref, ys_hbm, *rest, bt, n_steps, final_norm):
    if final_norm:
        g_ref, o_ref, buf, sems = rest
    else:
        o_ref, buf, sems = rest
    i = pl.program_id(0)
    slot = i % 2

    def issue(step, to_slot):
        def one(r, carry):
            t = step * bt + r
            _row_copy(ys_hbm, pos1_ref[t], buf.at[to_slot, 0], r, sems.at[to_slot]).start(priority=0)
            _row_copy(ys_hbm, pos2_ref[t], buf.at[to_slot, 1], r, sems.at[to_slot]).start(priority=1)
            return carry
        lax.fori_loop(0, bt, one, 0, unroll=8)

    @pl.when(i == 0)
    def _():
        issue(0, 0)

    @pl.when(i + 1 < n_steps)
    def _():
        issue(i + 1, 1 - slot)

    def drain(r, carry):
        _row_copy(ys_hbm, 0, buf.at[slot, 0], 0, sems.at[slot]).wait()
        _row_copy(ys_hbm, 0, buf.at[slot, 1], 0, sems.at[slot]).wait()
        return carry

    lax.fori_loop(0, bt, drain, 0, unroll=8)
    w1 = r_ref[:, _R_W1:_R_W1 + 1]
    w2 = r_ref[:, _R_W2:_R_W2 + 1]
    out = x_ref[...] + (w1 * buf[slot, 0] + w2 * buf[slot, 1])
    if final_norm:
        out = (out * lax.rsqrt(jnp.mean(out * out, axis=-1, keepdims=True) + EPS)) * g_ref[...]
    o_ref[...] = out


def _combine(x, ys, route, pos1, pos2, final_gain=None):
    n, d = x.shape
    bt = _tile(n, 256)
    n_steps = n // bt
    final_norm = final_gain is not None
    in_specs = [pl.BlockSpec((bt, LANES), lambda i, p1, p2: (i, 0)),
                pl.BlockSpec((bt, d), lambda i, p1, p2: (i, 0)),
                pl.BlockSpec(memory_space=pl.ANY)]
    args = [pos1, pos2, route, x, ys]
    if final_norm:
        in_specs.append(pl.BlockSpec((1, d), lambda i, p1, p2: (0, 0)))
        args.append(final_gain.astype(F32).reshape(1, d))
    return pl.pallas_call(
        functools.partial(_combine_body, bt=bt, n_steps=n_steps, final_norm=final_norm),
        out_shape=jax.ShapeDtypeStruct((n, d), F32),
        grid_spec=pltpu.PrefetchScalarGridSpec(
            num_scalar_prefetch=2,
            grid=(n_steps,),
            in_specs=in_specs,
            out_specs=pl.BlockSpec((bt, d), lambda i, p1, p2: (i, 0)),
            scratch_shapes=[pltpu.VMEM((2, TOP_K, bt, d), F32), pltpu.SemaphoreType.DMA((2,))]),
        compiler_params=_params(("arbitrary",), 2 * _nbytes((bt, d), F32),
                                scratch_bytes=_nbytes((2, TOP_K, bt, d), F32),
                                temp_bytes=2 * _nbytes((bt, d), F32)),
        name="moe_combine",
    )(*args)


def _moe(x, ln, router_w, router_b, w_gate_up, w_down, layer, final_gain, bm_pref=512):
    n, d = x.shape
    n_exp = router_w.shape[1]
    bm = _tile(n * TOP_K, bm_pref)
    n_tiles_max = (n * TOP_K) // bm + n_exp
    h_f32, logits = _rmsnorm_router(x, ln, router_w, router_b)
    route, counts = _route(logits, n_exp)
    counts = counts[0, :n_exp].astype(jnp.int32)
    sizes = (counts + bm - 1) // bm * bm
    ends = jnp.cumsum(sizes)
    starts = ends - sizes
    e1, e2 = route[:, _R_E1].astype(jnp.int32), route[:, _R_E2].astype(jnp.int32)
    pos1 = starts[e1] + route[:, _R_RANK1].astype(jnp.int32)
    pos2 = starts[e2] + route[:, _R_RANK2].astype(jnp.int32)
    n_tiles = (ends[-1] // bm).astype(jnp.int32).reshape(1)
    tile_start = jnp.arange(n_tiles_max, dtype=jnp.int32) * bm
    tile_expert = jnp.sum(tile_start[:, None] >= ends[None, :], axis=1).astype(jnp.int32)
    tile_expert = jnp.minimum(tile_expert, tile_expert[n_tiles[0] - 1])
    tile = jnp.arange(n_tiles_max, dtype=jnp.int32)
    later_other = ((tile[None, :] > tile[:, None]) & (tile[None, :] < n_tiles[0])
                   & (tile_expert[None, :] != tile_expert[:, None]))
    first_other = jnp.min(jnp.where(later_other, tile[None, :], n_tiles_max), axis=1)
    next_expert = jnp.where(first_other < n_tiles_max,
                            tile_expert[jnp.minimum(first_other, n_tiles_max - 1)], -1).astype(jnp.int32)
    tiles = (tile_expert, n_tiles, next_expert)

    n_slots = n_tiles_max * bm
    xs = _gather_rows(h_f32, _invert_slots(pos1, pos2, n_slots), n_slots)
    act = _moe_up(xs, w_gate_up, layer, tiles, bm)
    ys = _moe_down(act, w_down, layer, tiles, bm)
    return _combine(x, ys, route, pos1, pos2, final_gain)


def kernel(x, ln_mix, ln_ffn, w_in, conv_w, conv_b, i_bias, f_bias, mh_norm, p_conv, p_mlstm, w_out,
           ffn_w_gate_up, ffn_w_down, router_w, router_b, exp_w_gate_up, exp_w_down, final_norm):
    batch, seq, d = x.shape
    depth = ln_mix.shape[0]
    xf = x.reshape(batch * seq, d)
    wt_in = jnp.transpose(w_in, (0, 2, 1))
    normed = False
    for layer in range(depth):
        xf = _hybrid_mixer(xf, ln_mix[layer], wt_in, layer, conv_w[layer], conv_b[layer],
                           i_bias[layer], f_bias[layer], mh_norm[layer], p_conv, p_mlstm, w_out,
                           batch, seq)
        j = layer // 2
        if layer % 2 == 0:
            h = _rmsnorm(xf, ln_ffn[layer], BF16)
            act = _swiglu_up(h, ffn_w_gate_up, j)
            xf = _matmul_residual_ktiled(act, ffn_w_down[j].astype(BF16), xf)
        else:
            normed = layer == depth - 1
            xf = _moe(xf, ln_ffn[layer], router_w[j], router_b[j], exp_w_gate_up, exp_w_down, j,
                      final_norm if normed else None)
    if not normed:
        xf = _rmsnorm(xf, final_norm, x.dtype)
    return xf.reshape(batch, seq, d)
```
